```python
import jax, jax.numpy as jnp
from jax import lax
import numpy as np

D_MODEL = 1024
BATCH = 8
SEQ = 2048
DEPTH = 4

MIX_WIDTH = D_MODEL
POOL_WIDTH = MIX_WIDTH // 4
POOL_WINDOWS = (2, 4, 8, 16)
POOL_GROUPS = len(POOL_WINDOWS)
POOL_GROUP_DIM = POOL_WIDTH // POOL_GROUPS
HEAD_DIM = 64
ATTN_WIDTH = MIX_WIDTH - POOL_WIDTH
N_HEADS = ATTN_WIDTH // HEAD_DIM
DILATED_PATTERNS = ((128, 1), (512, 4), (2048, 16))
ROPE_THETA = 500000.0
ROPE_DIM = HEAD_DIM // 4
D_FF = 2816
IN_PROJ_WIDTH = POOL_WIDTH + 3 * ATTN_WIDTH
NORM_EPS = 1e-6
MASK_VALUE = -1e30

kernel_name = "hybrid_pool_dilated_attn_macaron_encoder"


def rmsnorm(x, g):
    xf = x.astype(jnp.float32)
    y = xf * lax.rsqrt(jnp.mean(xf * xf, axis=-1, keepdims=True) + NORM_EPS)
    return (y * g.astype(jnp.float32)).astype(x.dtype)


def swiglu(h, w_gate, w_up, w_down):
    return (jax.nn.silu(h @ w_gate) * (h @ w_up)) @ w_down


def rope_tables(positions):
    inv_freq = ROPE_THETA ** (-jnp.arange(0, ROPE_DIM, 2, dtype=jnp.float32) / ROPE_DIM)
    ang = positions.astype(jnp.float32)[..., None] * inv_freq
    return jnp.cos(ang)[:, :, None, :], jnp.sin(ang)[:, :, None, :]


def apply_partial_rope(t, cos, sin):
    tf = t.astype(jnp.float32)
    half = ROPE_DIM // 2
    t1, t2, rest = tf[..., :half], tf[..., half:ROPE_DIM], tf[..., ROPE_DIM:]
    rot = jnp.concatenate([t1 * cos - t2 * sin, t2 * cos + t1 * sin, rest], axis=-1)
    return rot.astype(t.dtype)


def multiscale_pool(v, pool_w, pool_scale):
    B, S, _ = v.shape
    vf = v.astype(jnp.float32).reshape(B, S, POOL_GROUPS, POOL_GROUP_DIM)
    cs = jnp.pad(lax.cumsum(vf, axis=1), ((0, 0), (1, 0), (0, 0), (0, 0)))
    pos = jnp.arange(S)
    means = []
    for g, w in enumerate(POOL_WINDOWS):
        lo = jnp.maximum(pos - w // 2, 0)
        hi = jnp.minimum(pos + w - 1 - w // 2, S - 1)
        cnt = (hi - lo + 1).astype(jnp.float32)
        means.append((cs[:, hi + 1, g] - cs[:, lo, g]) / cnt[None, :, None])
    pooled = jnp.stack(means, axis=2)
    diff = (pooled - vf).astype(v.dtype)
    y = jnp.einsum('bsgc,gcd->bsgd', diff, pool_w).reshape(B, S, POOL_WIDTH)
    return y * pool_scale


def dilated_branch(q, k, v, window, dilation):
    B, S, H, Dh = q.shape
    half = window // (2 * dilation)
    blk = half
    L = S // dilation
    nb = -(-L // blk)
    Lp = nb * blk

    def to_compressed(t):
        t = t.astype(jnp.float32).reshape(B, L, dilation, H, Dh)
        return jnp.pad(t, ((0, 0), (0, Lp - L), (0, 0), (0, 0), (0, 0)))

    def band(t):
        tp = jnp.pad(t, ((0, 0), (blk, blk), (0, 0), (0, 0), (0, 0)))
        tp = tp.reshape(B, nb + 2, blk, dilation, H, Dh)
        return jnp.concatenate([tp[:, :-2], tp[:, 1:-1], tp[:, 2:]], axis=2)

    qb = to_compressed(q).reshape(B, nb, blk, dilation, H, Dh)
    kb = band(to_compressed(k))
    vb = band(to_compressed(v))

    t_idx = jnp.arange(nb)[:, None] * blk + jnp.arange(blk)[None, :]
    j_idx = jnp.arange(nb)[:, None] * blk - blk + jnp.arange(3 * blk)[None, :]
    jj = j_idx[:, None, :]
    valid = (jnp.abs(jj - t_idx[:, :, None]) <= half) & (jj >= 0) & (jj < L)

    scale = 1.0 / np.sqrt(Dh)
    s = jnp.einsum('bnqrhd,bnkrhd->bnrhqk', qb, kb) * scale
    s = jnp.where(valid[None, :, None, None], s, MASK_VALUE)
    m = jnp.max(s, axis=-1, keepdims=True)
    p = jnp.exp(s - m)
    denom = jnp.sum(p, axis=-1)
    num = jnp.einsum('bnrhqk,bnkrhd->bnqrhd', p, vb)

    num = num.reshape(B, Lp, dilation, H, Dh)[:, :L].reshape(B, S, H, Dh)

    def stat_back(t):
        t = jnp.transpose(t, (0, 1, 4, 2, 3)).reshape(B, Lp, dilation, H)
        return t[:, :L].reshape(B, S, H)

    return num, stat_back(m[..., 0]), stat_back(denom)


def dilated_mixture_attention(q, k, v):
    branches = [dilated_branch(q, k, v, w, d) for (w, d) in DILATED_PATTERNS]
    m_all = jnp.stack([b[1] for b in branches], axis=0)
    wts = jnp.exp(m_all - jnp.max(m_all, axis=0, keepdims=True))
    num = sum(wts[i][..., None] * branches[i][0] for i in range(len(branches)))
    den = sum(wts[i] * branches[i][2] for i in range(len(branches)))
    return (num / den[..., None]).astype(q.dtype)


def setup_inputs(seed: int = 0) -> dict:
    key = jax.random.key(seed)
    ks = jax.random.split(key, 20)
    f32 = jnp.float32

    def normal(k, shape, fan_in):
        return jax.random.normal(k, shape, f32) * (fan_in ** -0.5)

    def gain(k, shape):
        return jnp.ones(shape, f32) + 0.02 * jax.random.normal(k, shape, f32)

    x = jax.random.normal(ks[0], (BATCH, SEQ, D_MODEL), f32)
    start = jax.random.randint(ks[1], (BATCH, 1), 0, 4096, dtype=jnp.int32)
    positions = start + jnp.arange(SEQ, dtype=jnp.int32)[None, :]
    return {
        "x": x,
        "positions": positions,
        "ffn1_norm": gain(ks[2], (DEPTH, D_MODEL)),
        "ffn1_w_gate": normal(ks[3], (DEPTH, D_MODEL, D_FF), D_MODEL),
        "ffn1_w_up": normal(ks[4], (DEPTH, D_MODEL, D_FF), D_MODEL),
        "ffn1_w_down": normal(ks[5], (DEPTH, D_FF, D_MODEL), D_FF),
        "mix_norm": gain(ks[6], (DEPTH, D_MODEL)),
        "w_in": normal(ks[7], (DEPTH, D_MODEL, IN_PROJ_WIDTH), D_MODEL),
        "pool_w": normal(ks[8], (DEPTH, POOL_GROUPS, POOL_GROUP_DIM, POOL_GROUP_DIM), POOL_GROUP_DIM),
        "pool_scale": gain(ks[9], (DEPTH, POOL_WIDTH)),
        "w_out": normal(ks[10], (DEPTH, MIX_WIDTH, D_MODEL), MIX_WIDTH),
        "ffn2_norm": gain(ks[11], (DEPTH, D_MODEL)),
        "ffn2_w_gate": normal(ks[12], (DEPTH, D_MODEL, D_FF), D_MODEL),
        "ffn2_w_up": normal(ks[13], (DEPTH, D_MODEL, D_FF), D_MODEL),
        "ffn2_w_down": normal(ks[14], (DEPTH, D_FF, D_MODEL), D_FF),
        "final_norm": gain(ks[15], (D_MODEL,)),
    }


def reference(x, positions, ffn1_norm, ffn1_w_gate, ffn1_w_up, ffn1_w_down, mix_norm, w_in,
              pool_w, pool_scale, w_out, ffn2_norm, ffn2_w_gate, ffn2_w_up, ffn2_w_down, final_norm):
    B, S, _ = x.shape
    cos, sin = rope_tables(positions)
    for l in range(DEPTH):
        x = x + 0.5 * swiglu(rmsnorm(x, ffn1_norm[l]), ffn1_w_gate[l], ffn1_w_up[l], ffn1_w_down[l])

        h = rmsnorm(x, mix_norm[l])
        proj = h @ w_in[l]
        v_pool = proj[..., :POOL_WIDTH]
        q = proj[..., POOL_WIDTH:POOL_WIDTH + ATTN_WIDTH].reshape(B, S, N_HEADS, HEAD_DIM)
        k = proj[..., POOL_WIDTH + ATTN_WIDTH:POOL_WIDTH + 2 * ATTN_WIDTH].reshape(B, S, N_HEADS, HEAD_DIM)
        v = proj[..., POOL_WIDTH + 2 * ATTN_WIDTH:].reshape(B, S, N_HEADS, HEAD_DIM)

        y_pool = multiscale_pool(v_pool, pool_w[l], pool_scale[l])
        q = apply_partial_rope(q, cos, sin)
        k = apply_partial_rope(k, cos, sin)
        y_attn = dilated_mixture_attention(q, k, v).reshape(B, S, ATTN_WIDTH)

        mixed = jnp.concatenate([y_pool.astype(x.dtype), y_attn.astype(x.dtype)], axis=-1)
        x = x + mixed @ w_out[l]

        x = x + 0.5 * swiglu(rmsnorm(x, ffn2_norm[l]), ffn2_w_gate[l], ffn2_w_up[l], ffn2_w_down[l])
    return rmsnorm(x, final_norm)
```

```python
import functools

import jax
import jax.numpy as jnp
from jax import lax
from jax.experimental import pallas as pl
from jax.experimental.pallas import tpu as pltpu

D_MODEL = 1024
DEPTH = 4
POOL_WIDTH = 256
POOL_WINDOWS = (2, 4, 8, 16)
POOL_GROUP_DIM = 64
HEAD_DIM = 64
ATTN_WIDTH = 768
DILATED_PATTERNS = ((128, 1), (512, 4), (2048, 16))
ROPE_THETA = 500000.0
ROPE_DIM = 16
D_FF = 2816
IN_PROJ_WIDTH = 2560
NORM_EPS = 1e-6
MASK_VALUE = -1e30

LANES = 128
ROW_TILE = 512
FF_CHUNK = 256
N_FF_CHUNKS = D_FF // FF_CHUNK
Q_BLOCK = 128
POOL_PAD = 16
VMEM_LIMIT = 56 * 1024 * 1024

_F32 = jnp.float32
_BF16 = jnp.bfloat16


def _params(n_grid_dims):
    return pltpu.CompilerParams(
        dimension_semantics=("arbitrary",) * n_grid_dims,
        vmem_limit_bytes=VMEM_LIMIT)


def _resident(block_shape, index_map):
    return pl.BlockSpec(block_shape, index_map, pipeline_mode=pl.Buffered(1))


def _rmsnorm(x, g):
    y = x * lax.rsqrt(jnp.mean(x * x, axis=-1, keepdims=True) + NORM_EPS)
    return y * g


def _ffn_kernel(x_ref, g_ref, wg_ref, wu_ref, wd_ref, fg_ref, o_ref, h_ref, acc_ref, *,
                apply_final_norm):
    x = x_ref[...]
    h_ref[...] = _rmsnorm(x, g_ref[...]).astype(_BF16)
    acc_ref[...] = jnp.zeros_like(acc_ref)

    def chunk(c, carry):
        h = h_ref[...]
        gate = jnp.dot(h, wg_ref[c], preferred_element_type=_F32)
        up = jnp.dot(h, wu_ref[c], preferred_element_type=_F32)
        act = ((gate * jax.nn.sigmoid(gate)) * up).astype(_BF16)
        acc_ref[...] += jnp.dot(act, wd_ref[c], preferred_element_type=_F32)
        return carry

    lax.fori_loop(0, N_FF_CHUNKS, chunk, 0)
    y = x + 0.5 * acc_ref[...]
    if apply_final_norm:
        y = _rmsnorm(y, fg_ref[...])
    o_ref[...] = y


def _ffn(x, gains, wg, wu, wd, final_gain, layer, apply_final_norm):
    n = x.shape[0]
    return pl.pallas_call(
        functools.partial(_ffn_kernel, apply_final_norm=apply_final_norm),
        grid=(n // ROW_TILE,),
        in_specs=[
            pl.BlockSpec((ROW_TILE, D_MODEL), lambda i: (i, 0)),
            pl.BlockSpec((None, 1, D_MODEL), lambda i: (layer, 0, 0)),
            _resident((None, N_FF_CHUNKS, D_MODEL, FF_CHUNK), lambda i: (layer, 0, 0, 0)),
            _resident((None, N_FF_CHUNKS, D_MODEL, FF_CHUNK), lambda i: (layer, 0, 0, 0)),
            _resident((None, N_FF_CHUNKS, FF_CHUNK, D_MODEL), lambda i: (layer, 0, 0, 0)),
            pl.BlockSpec((1, D_MODEL), lambda i: (0, 0)),
        ],
        out_specs=pl.BlockSpec((ROW_TILE, D_MODEL), lambda i: (i, 0)),
        out_shape=jax.ShapeDtypeStruct((n, D_MODEL), _F32),
        scratch_shapes=[pltpu.VMEM((ROW_TILE, D_MODEL), _BF16),
                        pltpu.VMEM((ROW_TILE, D_MODEL), _F32)],
        compiler_params=_params(1),
        name="ffn",
    )(x, gains, wg, wu, wd, final_gain)


def _inproj_kernel(x_ref, g_ref, w_ref, o_ref):
    h = _rmsnorm(x_ref[...], g_ref[...]).astype(_BF16)
    o_ref[...] = jnp.dot(h, w_ref[...], preferred_element_type=_F32)


def _inproj(x, gains, w_in, layer):
    n = x.shape[0]
    return pl.pallas_call(
        _inproj_kernel,
        grid=(n // ROW_TILE,),
        in_specs=[
            pl.BlockSpec((ROW_TILE, D_MODEL), lambda i: (i, 0)),
            pl.BlockSpec((None, 1, D_MODEL), lambda i: (layer, 0, 0)),
            _resident((None, D_MODEL, IN_PROJ_WIDTH), lambda i: (layer, 0, 0)),
        ],
        out_specs=pl.BlockSpec((ROW_TILE, IN_PROJ_WIDTH), lambda i: (i, 0)),
        out_shape=jax.ShapeDtypeStruct((n, IN_PROJ_WIDTH), _F32),
        compiler_params=_params(1),
        name="inproj",
    )(x, gains, w_in)


def _rope_table_kernel(pos_ref, invf_ref, cos_ref, sin_lo_ref, sin_hi_ref):
    ang = pos_ref[...] * invf_ref[...]
    c = jnp.cos(ang)
    s = jnp.sin(ang)
    d = lax.broadcasted_iota(jnp.int32, ang.shape, 1) & (HEAD_DIM - 1)
    half = ROPE_DIM // 2
    cos_ref[...] = jnp.where(d < ROPE_DIM, c, 1.0)
    sin_lo_ref[...] = jnp.where(d < half, -s, 0.0)
    sin_hi_ref[...] = jnp.where((d >= half) & (d < ROPE_DIM), s, 0.0)


def _rope_tables(pos, invf):
    b, s, _ = pos.shape
    out = jax.ShapeDtypeStruct((b, s, LANES), _F32)
    spec = pl.BlockSpec((None, s, LANES), lambda i: (i, 0, 0))
    return pl.pallas_call(
        _rope_table_kernel,
        grid=(b,),
        in_specs=[pl.BlockSpec((None, s, 1), lambda i: (i, 0, 0)),
                  pl.BlockSpec((1, LANES), lambda i: (0, 0))],
        out_specs=[spec, spec, spec],
        out_shape=[out, out, out],
        compiler_params=_params(1),
        name="rope_tables",
    )(pos, invf)


def _pool_kernel(v_ref, w_ref, scale_ref, o_ref, pad_ref):
    s = v_ref.shape[0]
    tile = pl.program_id(1)
    v = v_ref[...]
    pad_ref[...] = jnp.zeros_like(pad_ref)
    pad_ref[POOL_PAD:POOL_PAD + s, :] = v
    x = pad_ref[...]
    rows = x.shape[0]

    def shifted(t, k):
        return pltpu.roll(t, k % rows, 0)

    w2 = x + shifted(x, 1)
    w4 = shifted(w2, 1) + shifted(w2, -1)
    w8 = shifted(w4, 2) + shifted(w4, -2)
    w16 = shifted(w8, 4) + shifted(w8, -4)
    first_group = lax.broadcasted_iota(jnp.int32, x.shape, 1) < POOL_GROUP_DIM
    small = jnp.where(first_group, w2, w4)
    large = jnp.where(first_group, w8, w16)
    wsum = jnp.where(tile == 0, small, large)[POOL_PAD:POOL_PAD + s, :]

    lane_first = lax.broadcasted_iota(jnp.int32, v.shape, 1) < POOL_GROUP_DIM
    half_w = jnp.where(tile == 0, jnp.where(lane_first, 1, 2), jnp.where(lane_first, 4, 8))
    pos = lax.broadcasted_iota(jnp.int32, v.shape, 0)
    lo = jnp.maximum(pos - half_w, 0)
    hi = jnp.minimum(pos + half_w - 1, s - 1)
    cnt = (hi - lo + 1).astype(_F32)
    diff = wsum / cnt - v
    y = jnp.dot(diff.astype(_BF16), w_ref[...], preferred_element_type=_F32)
    o_ref[...] = y * scale_ref[...]


def _pool(proj, pool_w_bd, pool_scale, layer):
    b, s, _ = proj.shape
    n_tiles = POOL_WIDTH // LANES
    return pl.pallas_call(
        _pool_kernel,
        grid=(b, n_tiles),
        in_specs=[
            pl.BlockSpec((None, s, LANES), lambda i, j: (i, 0, j)),
            pl.BlockSpec((None, None, LANES, LANES), lambda i, j: (layer, j, 0, 0)),
            pl.BlockSpec((None, 1, LANES), lambda i, j: (layer, 0, j)),
        ],
        out_specs=pl.BlockSpec((None, s, LANES), lambda i, j: (i, 0, j)),
        out_shape=jax.ShapeDtypeStruct((b, s, POOL_WIDTH), _F32),
        scratch_shapes=[pltpu.VMEM((s + 2 * POOL_PAD, LANES), _F32)],
        compiler_params=_params(2),
        name="pool",
    )(proj, pool_w_bd, pool_scale)


def _attn_kernel(q_ref, k_ref, v_ref, cos_ref, sin_lo_ref, sin_hi_ref, o_ref,
                 qf_ref, kf_ref, qa_ref, qb_ref, kb_ref, vb_ref,
                 nump_ref, mp_ref, lp_ref, *nat_refs):
    s_len = q_ref.shape[0]
    cos = cos_ref[...]
    sin_lo = sin_lo_ref[...]
    sin_hi = sin_hi_ref[...]
    half_rope = ROPE_DIM // 2

    def rope(t):
        return (t * cos + pltpu.roll(t, LANES - half_rope, 1) * sin_lo
                + pltpu.roll(t, half_rope, 1) * sin_hi)

    qf_ref[...] = rope(q_ref[...]) * (1.0 / (HEAD_DIM ** 0.5))
    kf_ref[...] = rope(k_ref[...])

    for bi, (window, dil) in enumerate(DILATED_PATTERNS):
        seq = s_len // dil
        half = window // (2 * dil)
        kwin = min(2 * Q_BLOCK, seq)
        blocks_per_seq = seq // Q_BLOCK
        num_nat, m_nat, l_nat = nat_refs[3 * bi:3 * bi + 3]

        for r in range(dil):
            rows = pl.ds(r, seq, stride=dil) if dil > 1 else slice(None)
            dst = slice(r * seq, (r + 1) * seq)
            qr = qf_ref[rows, :]
            first = lax.broadcasted_iota(jnp.int32, qr.shape, 1) < HEAD_DIM
            qa_ref[dst, :] = jnp.where(first, qr, 0.0).astype(_BF16)
            qb_ref[dst, :] = jnp.where(first, 0.0, qr).astype(_BF16)
            kb_ref[dst, :] = kf_ref[rows, :].astype(_BF16)
            vb_ref[dst, :] = v_ref[rows, :].astype(_BF16)

        if dil == 1:
            num_dst, m_dst, l_dst = num_nat, m_nat, l_nat
        else:
            num_dst, m_dst, l_dst = nump_ref, mp_ref, lp_ref

        def block(i, carry, seq=seq, half=half, kwin=kwin, blocks_per_seq=blocks_per_seq,
                  num_dst=num_dst, m_dst=m_dst, l_dst=l_dst):
            q0 = (i % blocks_per_seq) * Q_BLOCK
            base = (i // blocks_per_seq) * seq
            k0 = jnp.clip(q0 - half, 0, seq - kwin)
            qrow = pl.multiple_of(base + q0, Q_BLOCK)
            krow = pl.multiple_of(base + k0, half)
            kblk = kb_ref[pl.ds(krow, kwin), :]
            vblk = vb_ref[pl.ds(krow, kwin), :]
            rr = lax.broadcasted_iota(jnp.int32, (Q_BLOCK, kwin), 0)
            cc = lax.broadcasted_iota(jnp.int32, (Q_BLOCK, kwin), 1)
            valid = jnp.abs(cc - rr + (k0 - q0)) <= half
            res = []
            for q_src in (qa_ref, qb_ref):
                qh = q_src[pl.ds(qrow, Q_BLOCK), :]
                sc = lax.dot_general(qh, kblk, (((1,), (1,)), ((), ())),
                                     preferred_element_type=_F32)
                sc = jnp.where(valid, sc, MASK_VALUE)
                m = jnp.max(sc, axis=-1, keepdims=True)
                p = jnp.exp(sc - m)
                l = jnp.sum(p, axis=-1, keepdims=True)
                o = jnp.dot(p.astype(_BF16), vblk, preferred_element_type=_F32)
                res.append((o, m, l))
            first = lax.broadcasted_iota(jnp.int32, (Q_BLOCK, LANES), 1) < HEAD_DIM
            out_rows = pl.ds(qrow, Q_BLOCK)
            num_dst[out_rows, :] = jnp.where(first, res[0][0], res[1][0])
            m_dst[out_rows, :] = jnp.where(first, res[0][1], res[1][1])
            l_dst[out_rows, :] = jnp.where(first, res[0][2], res[1][2])
            return carry

        lax.fori_loop(0, s_len // Q_BLOCK, block, 0)

        if dil > 1:
            for r in range(dil):
                rows = pl.ds(r, seq, stride=dil)
                src = slice(r * seq, (r + 1) * seq)
                num_nat[rows, :] = nump_ref[src, :]
                m_nat[rows, :] = mp_ref[src, :]
                l_nat[rows, :] = lp_ref[src, :]

    def merge(i, carry):
        rows = pl.ds(pl.multiple_of(i * Q_BLOCK, Q_BLOCK), Q_BLOCK)
        ms = [nat_refs[3 * b + 1][rows, :] for b in range(len(DILATED_PATTERNS))]
        m_all = functools.reduce(jnp.maximum, ms)
        num = 0.0
        den = 0.0
        for b in range(len(DILATED_PATTERNS)):
            w = jnp.exp(ms[b] - m_all)
            num = num + w * nat_refs[3 * b][rows, :]
            den = den + w * nat_refs[3 * b + 2][rows, :]
        o_ref[rows, :] = num / den
        return carry

    lax.fori_loop(0, s_len // Q_BLOCK, merge, 0)


def _attention(proj, cos, sin_lo, sin_hi):
    b, s, _ = proj.shape
    n_pairs = ATTN_WIDTH // LANES
    q_off = POOL_WIDTH // LANES
    k_off = q_off + n_pairs
    v_off = k_off + n_pairs

    def tile(off):
        return pl.BlockSpec((None, s, LANES), lambda i, j: (i, 0, off + j))

    table = pl.BlockSpec((None, s, LANES), lambda i, j: (i, 0, 0))
    f32_buf = pltpu.VMEM((s, LANES), _F32)
    bf16_buf = pltpu.VMEM((s, LANES), _BF16)
    return pl.pallas_call(
        _attn_kernel,
        grid=(b, n_pairs),
        in_specs=[tile(q_off), tile(k_off), tile(v_off), table, table, table],
        out_specs=pl.BlockSpec((None, s, LANES), lambda i, j: (i, 0, j)),
        out_shape=jax.ShapeDtypeStruct((b, s, ATTN_WIDTH), _F32),
        scratch_shapes=[f32_buf] * 2 + [bf16_buf] * 4 + [f32_buf] * 3
        + [f32_buf] * (3 * len(DILATED_PATTERNS)),
        compiler_params=_params(2),
        name="dilated_attention",
    )(proj, proj, proj, cos, sin_lo, sin_hi)


def _outproj_kernel(x_ref, yp_ref, ya_ref, w_ref, o_ref):
    acc = jnp.dot(yp_ref[...].astype(_BF16), w_ref[0:POOL_WIDTH, :],
                  preferred_element_type=_F32)
    acc = acc + jnp.dot(ya_ref[...].astype(_BF16), w_ref[POOL_WIDTH:D_MODEL, :],
                        preferred_element_type=_F32)
    o_ref[...] = x_ref[...] + acc


def _outproj(x, y_pool, y_attn, w_out, layer):
    n = x.shape[0]
    return pl.pallas_call(
        _outproj_kernel,
        grid=(n // ROW_TILE,),
        in_specs=[
            pl.BlockSpec((ROW_TILE, D_MODEL), lambda i: (i, 0)),
            pl.BlockSpec((ROW_TILE, POOL_WIDTH), lambda i: (i, 0)),
            pl.BlockSpec((ROW_TILE, ATTN_WIDTH), lambda i: (i, 0)),
            _resident((None, D_MODEL, D_MODEL), lambda i: (layer, 0, 0)),
        ],
        out_specs=pl.BlockSpec((ROW_TILE, D_MODEL), lambda i: (i, 0)),
        out_shape=jax.ShapeDtypeStruct((n, D_MODEL), _F32),
        compiler_params=_params(1),
        name="outproj",
    )(x, y_pool, y_attn, w_out)


def _chunk_columns(w):
    w = w.reshape(DEPTH, D_MODEL, N_FF_CHUNKS, FF_CHUNK)
    return jnp.transpose(w, (0, 2, 1, 3)).astype(_BF16)


def _chunk_rows(w):
    return w.reshape(DEPTH, N_FF_CHUNKS, FF_CHUNK, D_MODEL).astype(_BF16)


def _pool_block_diag(pool_w):
    g = POOL_GROUP_DIM
    tiles = POOL_WIDTH // LANES
    per_tile = LANES // g
    z = jnp.zeros((DEPTH, tiles, LANES, LANES), pool_w.dtype)
    for t in range(tiles):
        for h in range(per_tile):
            z = z.at[:, t, h * g:(h + 1) * g, h * g:(h + 1) * g].set(pool_w[:, per_tile * t + h])
    return z.astype(_BF16)


@jax.jit
def _forward(x, positions, ffn1_norm, ffn1_w_gate, ffn1_w_up, ffn1_w_down, mix_norm, w_in,
             pool_w, pool_scale, w_out, ffn2_norm, ffn2_w_gate, ffn2_w_up, ffn2_w_down,
             final_norm):
    b, s, d = x.shape
    n = b * s
    assert d == D_MODEL and n % ROW_TILE == 0 and s % (Q_BLOCK * 16) == 0

    wg1, wu1, wd1 = _chunk_columns(ffn1_w_gate), _chunk_columns(ffn1_w_up), _chunk_rows(ffn1_w_down)
    wg2, wu2, wd2 = _chunk_columns(ffn2_w_gate), _chunk_columns(ffn2_w_up), _chunk_rows(ffn2_w_down)
    w_in_b = w_in.astype(_BF16)
    w_out_b = w_out.astype(_BF16)
    pool_w_bd = _pool_block_diag(pool_w)
    g1 = ffn1_norm.reshape(DEPTH, 1, D_MODEL)
    g2 = ffn2_norm.reshape(DEPTH, 1, D_MODEL)
    gm = mix_norm.reshape(DEPTH, 1, D_MODEL)
    ps = pool_scale.reshape(DEPTH, 1, POOL_WIDTH)
    gf = final_norm.reshape(1, D_MODEL)

    inv_freq = ROPE_THETA ** (-jnp.arange(0, ROPE_DIM, 2, dtype=_F32) / ROPE_DIM)
    invf = jnp.tile(inv_freq, LANES // (ROPE_DIM // 2)).reshape(1, LANES)
    pos = positions.astype(_F32)[..., None]
    cos, sin_lo, sin_hi = _rope_tables(pos, invf)

    xs = x.reshape(n, D_MODEL)
    for layer in range(DEPTH):
        xs = _ffn(xs, g1, wg1, wu1, wd1, gf, layer, False)
        proj = _inproj(xs, gm, w_in_b, layer).reshape(b, s, IN_PROJ_WIDTH)
        y_pool = _pool(proj, pool_w_bd, ps, layer)
        y_attn = _attention(proj, cos, sin_lo, sin_hi)
        xs = _outproj(xs, y_pool.reshape(n, POOL_WIDTH), y_attn.reshape(n, ATTN_WIDTH),
                      w_out_b, layer)
        xs = _ffn(xs, g2, wg2, wu2, wd2, gf, layer, layer == DEPTH - 1)
    return xs.reshape(b, s, D_MODEL)


def kernel(x, positions, ffn1_norm, ffn1_w_gate, ffn1_w_up, ffn1_w_down, mix_norm, w_in,
           pool_w, pool_scale, w_out, ffn2_norm, ffn2_w_gate, ffn2_w_up, ffn2_w_down,
           final_norm):
    return _forward(x, positions, ffn1_norm, ffn1_w_gate, ffn1_w_up, ffn1_w_down, mix_norm,
                    w_in, pool_w, pool_scale, w_out, ffn2_norm, ffn2_w_gate, ffn2_w_up,
                    ffn2_w_down, final_norm)
```

```python
import functools

import jax
import jax.numpy as jnp
from jax import lax
from jax.experimental import pallas as pl
from jax.experimental.pallas import tpu as pltpu

D_MODEL = 1024
DEPTH = 4
POOL_WIDTH = 256
POOL_WINDOWS = (2, 4, 8, 16)
POOL_GROUP_DIM = 64
HEAD_DIM = 64
ATTN_WIDTH = 768
DILATED_PATTERNS = ((128, 1), (512, 4), (2048, 16))
ROPE_THETA = 500000.0
ROPE_DIM = 16
D_FF = 2816
IN_PROJ_WIDTH = 2560
NORM_EPS = 1e-6
MASK_VALUE = -1e30

LANES = 128
MXU_WIDTH = 256
ROW_TILE = 512
FF_CHUNK = MXU_WIDTH
N_FF_CHUNKS = D_FF // FF_CHUNK
Q_BLOCK = 128
BAND_HALF = 64
MID_DIL = 4
QKV_WIDTH = 3 * ATTN_WIDTH
POOL_PAD = 16
VMEM_LIMIT = 56 * 1024 * 1024

_F32 = jnp.float32
_BF16 = jnp.bfloat16

assert all(w // (2 * d) == BAND_HALF for w, d in DILATED_PATTERNS)
assert [d for _, d in DILATED_PATTERNS] == [1, MID_DIL, MID_DIL * MID_DIL]


def _params(n_grid_dims):
    return pltpu.CompilerParams(
        dimension_semantics=("arbitrary",) * n_grid_dims,
        vmem_limit_bytes=VMEM_LIMIT)


def _resident(block_shape, index_map):
    return pl.BlockSpec(block_shape, index_map, pipeline_mode=pl.Buffered(1))


def _rmsnorm(x, g):
    y = x * lax.rsqrt(jnp.mean(x * x, axis=-1, keepdims=True) + NORM_EPS)
    return y * g


def _ffn_kernel(x_ref, g_ref, wg_ref, wu_ref, wd_ref, fg_ref, o_ref, h_ref, acc_ref, *,
                apply_final_norm):
    x = x_ref[...]
    h_ref[...] = _rmsnorm(x, g_ref[...]).astype(_BF16)
    acc_ref[...] = jnp.zeros_like(acc_ref)

    def chunk(c, carry):
        h = h_ref[...]
        gate = jnp.dot(h, wg_ref[c], preferred_element_type=_F32)
        up = jnp.dot(h, wu_ref[c], preferred_element_type=_F32)
        act = ((gate * jax.nn.sigmoid(gate)) * up).astype(_BF16)
        acc_ref[...] += jnp.dot(act, wd_ref[c], preferred_element_type=_F32)
        return carry

    lax.fori_loop(0, N_FF_CHUNKS, chunk, 0)
    y = x + 0.5 * acc_ref[...]
    if apply_final_norm:
        y = _rmsnorm(y, fg_ref[...])
    o_ref[...] = y


def _ffn(x, gains, wg, wu, wd, final_gain, layer, apply_final_norm):
    n = x.shape[0]
    return pl.pallas_call(
        functools.partial(_ffn_kernel, apply_final_norm=apply_final_norm),
        grid=(n // ROW_TILE,),
        in_specs=[
            pl.BlockSpec((ROW_TILE, D_MODEL), lambda i: (i, 0)),
            pl.BlockSpec((None, 1, D_MODEL), lambda i: (layer, 0, 0)),
            _resident((None, N_FF_CHUNKS, D_MODEL, FF_CHUNK), lambda i: (layer, 0, 0, 0)),
            _resident((None, N_FF_CHUNKS, D_MODEL, FF_CHUNK), lambda i: (layer, 0, 0, 0)),
            _resident((None, N_FF_CHUNKS, FF_CHUNK, D_MODEL), lambda i: (layer, 0, 0, 0)),
            pl.BlockSpec((1, D_MODEL), lambda i: (0, 0)),
        ],
        out_specs=pl.BlockSpec((ROW_TILE, D_MODEL), lambda i: (i, 0)),
        out_shape=jax.ShapeDtypeStruct((n, D_MODEL), _F32),
        scratch_shapes=[pltpu.VMEM((ROW_TILE, D_MODEL), _BF16),
                        pltpu.VMEM((ROW_TILE, D_MODEL), _F32)],
        compiler_params=_params(1),
        name="ffn",
    )(x, gains, wg, wu, wd, final_gain)


def _rope_table_kernel(pos_ref, invf_ref, cos_ref, sin_lo_ref, sin_hi_ref):
    ang = pos_ref[...] * invf_ref[...]
    c = jnp.cos(ang)
    s = jnp.sin(ang)
    d = lax.broadcasted_iota(jnp.int32, ang.shape, 1) & (HEAD_DIM - 1)
    half = ROPE_DIM // 2
    cos_ref[...] = jnp.where(d < ROPE_DIM, c, 1.0)
    sin_lo_ref[...] = jnp.where(d < half, -s, 0.0)
    sin_hi_ref[...] = jnp.where((d >= half) & (d < ROPE_DIM), s, 0.0)


def _rope_tables(pos, invf):
    b, s, _ = pos.shape
    out = jax.ShapeDtypeStruct((b, s, LANES), _F32)
    spec = pl.BlockSpec((None, s, LANES), lambda i: (i, 0, 0))
    return pl.pallas_call(
        _rope_table_kernel,
        grid=(b,),
        in_specs=[pl.BlockSpec((None, s, 1), lambda i: (i, 0, 0)),
                  pl.BlockSpec((1, LANES), lambda i: (0, 0))],
        out_specs=[spec, spec, spec],
        out_shape=[out, out, out],
        compiler_params=_params(1),
        name="rope_tables",
    )(pos, invf)


def _inproj_kernel(x_ref, g_ref, w_ref, cos_ref, sin_lo_ref, sin_hi_ref,
                   vpool_ref, nat_ref, mid_ref, far_ref, slab_ref):
    h = _rmsnorm(x_ref[...], g_ref[...]).astype(_BF16)
    cos = cos_ref[...]
    sin_lo = sin_lo_ref[...]
    sin_hi = sin_hi_ref[...]
    half_rope = ROPE_DIM // 2
    n_heads_tiles = ATTN_WIDTH // LANES

    def rope(t):
        return (t * cos + pltpu.roll(t, LANES - half_rope, 1) * sin_lo
                + pltpu.roll(t, half_rope, 1) * sin_hi)

    for c in range(IN_PROJ_WIDTH // MXU_WIDTH):
        cols = slice(c * MXU_WIDTH, (c + 1) * MXU_WIDTH)
        part = jnp.dot(h, w_ref[:, cols], preferred_element_type=_F32)
        if c * MXU_WIDTH < POOL_WIDTH:
            vpool_ref[:, cols] = part
            continue
        for half in range(MXU_WIDTH // LANES):
            j = (c * MXU_WIDTH - POOL_WIDTH) // LANES + half
            t = part[:, half * LANES:(half + 1) * LANES]
            if j < n_heads_tiles:
                t = rope(t) * (1.0 / (HEAD_DIM ** 0.5))
            elif j < 2 * n_heads_tiles:
                t = rope(t)
            slab_ref[j] = t
            nat_ref[:, j * LANES:(j + 1) * LANES] = t.astype(_BF16)

    far_dil = MID_DIL * MID_DIL
    for j in range(QKV_WIDTH // LANES):
        lanes = slice(j * LANES, (j + 1) * LANES)
        for r in range(MID_DIL):
            rows = pl.ds(r, ROW_TILE // MID_DIL, stride=MID_DIL)
            mid_ref[r, :, lanes] = slab_ref[j, rows, :].astype(_BF16)
        for r in range(far_dil):
            rows = pl.ds(r, ROW_TILE // far_dil, stride=far_dil)
            far_ref[r, :, lanes] = slab_ref[j, rows, :].astype(_BF16)


def _inproj(x, gains, w_in, cos, sin_lo, sin_hi, layer):
    b, s, _ = x.shape
    tiles = s // ROW_TILE
    far_dil = MID_DIL * MID_DIL
    table = pl.BlockSpec((None, ROW_TILE, LANES), lambda i, t: (i, t, 0))
    return pl.pallas_call(
        _inproj_kernel,
        grid=(b, tiles),
        in_specs=[
            pl.BlockSpec((None, ROW_TILE, D_MODEL), lambda i, t: (i, t, 0)),
            pl.BlockSpec((None, 1, D_MODEL), lambda i, t: (layer, 0, 0)),
            _resident((None, D_MODEL, IN_PROJ_WIDTH), lambda i, t: (layer, 0, 0)),
            table, table, table,
        ],
        out_specs=[
            pl.BlockSpec((None, ROW_TILE, POOL_WIDTH), lambda i, t: (i, t, 0)),
            pl.BlockSpec((None, ROW_TILE, QKV_WIDTH), lambda i, t: (i, t, 0)),
            pl.BlockSpec((None, MID_DIL, ROW_TILE // MID_DIL, QKV_WIDTH),
                         lambda i, t: (i, 0, t, 0)),
            pl.BlockSpec((None, far_dil, ROW_TILE // far_dil, QKV_WIDTH),
                         lambda i, t: (i, 0, t, 0)),
        ],
        out_shape=[
            jax.ShapeDtypeStruct((b, s, POOL_WIDTH), _F32),
            jax.ShapeDtypeStruct((b, s, QKV_WIDTH), _BF16),
            jax.ShapeDtypeStruct((b, MID_DIL, s // MID_DIL, QKV_WIDTH), _BF16),
            jax.ShapeDtypeStruct((b, far_dil, s // far_dil, QKV_WIDTH), _BF16),
        ],
        scratch_shapes=[pltpu.VMEM((QKV_WIDTH // LANES, ROW_TILE, LANES), _F32)],
        compiler_params=_params(2),
        name="inproj",
    )(x, gains, w_in, cos, sin_lo, sin_hi)


def _pool_kernel(v_ref, w_ref, scale_ref, o_ref, pad_ref):
    s = v_ref.shape[0]
    tile = pl.program_id(1)
    v = v_ref[...]
    pad_ref[...] = jnp.zeros_like(pad_ref)
    pad_ref[POOL_PAD:POOL_PAD + s, :] = v
    x = pad_ref[...]
    rows = x.shape[0]

    def shifted(t, k):
        return pltpu.roll(t, k % rows, 0)

    w2 = x + shifted(x, 1)
    w4 = shifted(w2, 1) + shifted(w2, -1)
    w8 = shifted(w4, 2) + shifted(w4, -2)
    w16 = shifted(w8, 4) + shifted(w8, -4)
    first_group = lax.broadcasted_iota(jnp.int32, x.shape, 1) < POOL_GROUP_DIM
    small = jnp.where(first_group, w2, w4)
    large = jnp.where(first_group, w8, w16)
    wsum = jnp.where(tile == 0, small, large)[POOL_PAD:POOL_PAD + s, :]

    lane_first = lax.broadcasted_iota(jnp.int32, v.shape, 1) < POOL_GROUP_DIM
    half_w = jnp.where(tile == 0, jnp.where(lane_first, 1, 2), jnp.where(lane_first, 4, 8))
    pos = lax.broadcasted_iota(jnp.int32, v.shape, 0)
    lo = jnp.maximum(pos - half_w, 0)
    hi = jnp.minimum(pos + half_w - 1, s - 1)
    cnt = (hi - lo + 1).astype(_F32)
    diff = wsum / cnt - v
    y = jnp.dot(diff.astype(_BF16), w_ref[...], preferred_element_type=_F32)
    o_ref[...] = (y * scale_ref[...]).astype(o_ref.dtype)


def _pool(v_pool, pool_w_bd, pool_scale, layer):
    b, s, _ = v_pool.shape
    n_tiles = POOL_WIDTH // LANES
    return pl.pallas_call(
        _pool_kernel,
        grid=(b, n_tiles),
        in_specs=[
            pl.BlockSpec((None, s, LANES), lambda i, j: (i, 0, j)),
            pl.BlockSpec((None, None, LANES, LANES), lambda i, j: (layer, j, 0, 0)),
            pl.BlockSpec((None, 1, LANES), lambda i, j: (layer, 0, j)),
        ],
        out_specs=pl.BlockSpec((None, s, LANES), lambda i, j: (i, 0, j)),
        out_shape=jax.ShapeDtypeStruct((b, s, POOL_WIDTH), _BF16),
        scratch_shapes=[pltpu.VMEM((s + 2 * POOL_PAD, LANES), _F32)],
        compiler_params=_params(2),
        name="pool",
    )(v_pool, pool_w_bd, pool_scale)


def _band_block(q_ref, k_ref, v_ref, mask_ref, sq, q0, seq):
    kwin = min(2 * Q_BLOCK, seq)
    k0 = min(max(q0 - BAND_HALF, 0), seq - kwin)
    q = q_ref[sq, q0:q0 + Q_BLOCK, :]
    kblk = k_ref[sq, k0:k0 + kwin, :]
    vblk = v_ref[sq, k0:k0 + kwin, :]
    valid = mask_ref[(k0 - q0) // BAND_HALF + 2, :, 0:kwin] != 0.0
    first = lax.broadcasted_iota(jnp.int32, (Q_BLOCK, LANES), 1) < HEAD_DIM
    zero = jnp.zeros_like(q)
    res = []
    for qh in (jnp.where(first, q, zero), jnp.where(first, zero, q)):
        sc = lax.dot_general(qh, kblk, (((1,), (1,)), ((), ())), preferred_element_type=_F32)
        sc = jnp.where(valid, sc, MASK_VALUE)
        m = jnp.max(sc, axis=-1, keepdims=True)
        p = jnp.exp(sc - m)
        l = jnp.sum(p, axis=-1, keepdims=True)
        o = jnp.dot(p.astype(_BF16), vblk, preferred_element_type=_F32)
        res.append((o, m, l))
    return tuple(jnp.where(first, a, b) for a, b in zip(*res))


def _attn_kernel(qn_ref, kn_ref, vn_ref, qm_ref, km_ref, vm_ref, qf_ref, kf_ref, vf_ref, o_ref,
                 mask_ref, num_far, m_far, l_far, num_mid, m_mid, l_mid):
    s_len = qn_ref.shape[1]
    far_dil = MID_DIL * MID_DIL

    rr = lax.broadcasted_iota(jnp.int32, (Q_BLOCK, 2 * Q_BLOCK), 0)
    cc = lax.broadcasted_iota(jnp.int32, (Q_BLOCK, 2 * Q_BLOCK), 1)
    for idx in range(3):
        off = (idx - 2) * BAND_HALF
        mask_ref[idx] = (jnp.abs(cc - rr + off) <= BAND_HALF).astype(_F32)

    seq = s_len // far_dil
    for r in range(far_dil):
        for q0 in range(0, seq, Q_BLOCK):
            num, m, l = _band_block(qf_ref, kf_ref, vf_ref, mask_ref, r, q0, seq)
            rows = pl.ds(MID_DIL * q0 + r // MID_DIL, Q_BLOCK, stride=MID_DIL)
            num_far[r % MID_DIL, rows, :] = num
            m_far[r % MID_DIL, rows, :] = m
            l_far[r % MID_DIL, rows, :] = l

    seq = s_len // MID_DIL
    for r in range(MID_DIL):
        for q0 in range(0, seq, Q_BLOCK):
            num, m, l = _band_block(qm_ref, km_ref, vm_ref, mask_ref, r, q0, seq)
            rows = slice(q0, q0 + Q_BLOCK)
            m_c = m_far[r, rows, :]
            m_new = jnp.maximum(m, m_c)
            a = jnp.exp(m - m_new)
            a_c = jnp.exp(m_c - m_new)
            out_rows = pl.ds(MID_DIL * q0 + r, Q_BLOCK, stride=MID_DIL)
            num_mid[out_rows, :] = a * num + a_c * num_far[r, rows, :]
            l_mid[out_rows, :] = a * l + a_c * l_far[r, rows, :]
            m_mid[out_rows, :] = m_new

    for q0 in range(0, s_len, Q_BLOCK):
        num, m, l = _band_block(qn_ref, kn_ref, vn_ref, mask_ref, 0, q0, s_len)
        rows = slice(q0, q0 + Q_BLOCK)
        m_c = m_mid[rows, :]
        m_new = jnp.maximum(m, m_c)
        a = jnp.exp(m - m_new)
        a_c = jnp.exp(m_c - m_new)
        total = a * num + a_c * num_mid[rows, :]
        den = a * l + a_c * l_mid[rows, :]
        o_ref[rows, :] = (total / den).astype(o_ref.dtype)


def _attention(nat, mid, far):
    b, s, _ = nat.shape
    n_pairs = ATTN_WIDTH // LANES
    far_dil = MID_DIL * MID_DIL

    def specs(n_seq):
        def tile(off):
            return pl.BlockSpec((None, n_seq, s // n_seq, LANES),
                                lambda i, j: (i, 0, 0, off + j))
        return [tile(0), tile(n_pairs), tile(2 * n_pairs)]

    return pl.pallas_call(
        _attn_kernel,
        grid=(b, n_pairs),
        in_specs=specs(1) + specs(MID_DIL) + specs(far_dil),
        out_specs=pl.BlockSpec((None, s, LANES), lambda i, j: (i, 0, j)),
        out_shape=jax.ShapeDtypeStruct((b, s, ATTN_WIDTH), _BF16),
        scratch_shapes=[pltpu.VMEM((3, Q_BLOCK, 2 * Q_BLOCK), _F32)]
        + [pltpu.VMEM((MID_DIL, s // MID_DIL, LANES), _F32)] * 3
        + [pltpu.VMEM((s, LANES), _F32)] * 3,
        compiler_params=_params(2),
        name="dilated_attention",
    )(nat.reshape(b, 1, s, QKV_WIDTH), nat.reshape(b, 1, s, QKV_WIDTH),
      nat.reshape(b, 1, s, QKV_WIDTH), mid, mid, mid, far, far, far)


def _outproj_kernel(x_ref, yp_ref, ya_ref, w_ref, o_ref):
    acc = jnp.dot(yp_ref[...], w_ref[0:POOL_WIDTH, :], preferred_element_type=_F32)
    acc = acc + jnp.dot(ya_ref[...], w_ref[POOL_WIDTH:D_MODEL, :], preferred_element_type=_F32)
    o_ref[...] = x_ref[...] + acc


def _outproj(x, y_pool, y_attn, w_out, layer):
    n = x.shape[0]
    return pl.pallas_call(
        _outproj_kernel,
        grid=(n // ROW_TILE,),
        in_specs=[
            pl.BlockSpec((ROW_TILE, D_MODEL), lambda i: (i, 0)),
            pl.BlockSpec((ROW_TILE, POOL_WIDTH), lambda i: (i, 0)),
            pl.BlockSpec((ROW_TILE, ATTN_WIDTH), lambda i: (i, 0)),
            _resident((None, D_MODEL, D_MODEL), lambda i: (layer, 0, 0)),
        ],
        out_specs=pl.BlockSpec((ROW_TILE, D_MODEL), lambda i: (i, 0)),
        out_shape=jax.ShapeDtypeStruct((n, D_MODEL), _F32),
        compiler_params=_params(1),
        name="outproj",
    )(x, y_pool, y_attn, w_out)


def _chunk_columns(w):
    w = w.reshape(DEPTH, D_MODEL, N_FF_CHUNKS, FF_CHUNK)
    return jnp.transpose(w, (0, 2, 1, 3)).astype(_BF16)


def _chunk_rows(w):
    return w.reshape(DEPTH, N_FF_CHUNKS, FF_CHUNK, D_MODEL).astype(_BF16)


def _pool_block_diag(pool_w):
    g = POOL_GROUP_DIM
    tiles = POOL_WIDTH // LANES
    per_tile = LANES // g
    z = jnp.zeros((DEPTH, tiles, LANES, LANES), pool_w.dtype)
    for t in range(tiles):
        for h in range(per_tile):
            z = z.at[:, t, h * g:(h + 1) * g, h * g:(h + 1) * g].set(pool_w[:, per_tile * t + h])
    return z.astype(_BF16)


@jax.jit
def _forward(x, positions, ffn1_norm, ffn1_w_gate, ffn1_w_up, ffn1_w_down, mix_norm, w_in,
             pool_w, pool_scale, w_out, ffn2_norm, ffn2_w_gate, ffn2_w_up, ffn2_w_down,
             final_norm):
    b, s, d = x.shape
    n = b * s
    assert d == D_MODEL and s % ROW_TILE == 0 and s % (Q_BLOCK * MID_DIL * MID_DIL) == 0
    assert all(w // 2 == POOL_WINDOWS[0] * 2 ** i // 2 for i, w in enumerate(POOL_WINDOWS))

    wg1, wu1, wd1 = _chunk_columns(ffn1_w_gate), _chunk_columns(ffn1_w_up), _chunk_rows(ffn1_w_down)
    wg2, wu2, wd2 = _chunk_columns(ffn2_w_gate), _chunk_columns(ffn2_w_up), _chunk_rows(ffn2_w_down)
    w_in_b = w_in.astype(_BF16)
    w_out_b = w_out.astype(_BF16)
    pool_w_bd = _pool_block_diag(pool_w)
    g1 = ffn1_norm.reshape(DEPTH, 1, D_MODEL)
    g2 = ffn2_norm.reshape(DEPTH, 1, D_MODEL)
    gm = mix_norm.reshape(DEPTH, 1, D_MODEL)
    ps = pool_scale.reshape(DEPTH, 1, POOL_WIDTH)
    gf = final_norm.reshape(1, D_MODEL)

    inv_freq = ROPE_THETA ** (-jnp.arange(0, ROPE_DIM, 2, dtype=_F32) / ROPE_DIM)
    invf = jnp.tile(inv_freq, LANES // (ROPE_DIM // 2)).reshape(1, LANES)
    pos = positions.astype(_F32)[..., None]
    cos, sin_lo, sin_hi = _rope_tables(pos, invf)

    xs = x.reshape(n, D_MODEL)
    for layer in range(DEPTH):
        xs = _ffn(xs, g1, wg1, wu1, wd1, gf, layer, False)
        v_pool, nat, mid, far = _inproj(xs.reshape(b, s, D_MODEL), gm, w_in_b, cos, sin_lo,
                                        sin_hi, layer)
        y_pool = _pool(v_pool, pool_w_bd, ps, layer)
        y_attn = _attention(nat, mid, far)
        xs = _outproj(xs, y_pool.reshape(n, POOL_WIDTH), y_attn.reshape(n, ATTN_WIDTH),
                      w_out_b, layer)
        xs = _ffn(xs, g2, wg2, wu2, wd2, gf, layer, layer == DEPTH - 1)
    return xs.reshape(b, s, D_MODEL)


def kernel(x, positions, ffn1_norm, ffn1_w_gate, ffn1_w_up, ffn1_w_down, mix_norm, w_in,
           pool_w, pool_scale, w_out, ffn2_norm, ffn2_w_gate, ffn2_w_up, ffn2_w_down,
           final_norm):
    return _forward(x, positions, ffn1_norm, ffn1_w_gate, ffn1_w_up, ffn1_w_down, mix_norm,
                    w_in, pool_w, pool_scale, w_out, ffn2_norm, ffn2_w_gate, ffn2_w_up,
                    ffn2_w_down, final_norm)
```

```python
import functools

import jax
import jax.numpy as jnp
from jax import lax
from jax.experimental import pallas as pl
from jax.experimental.pallas import tpu as pltpu

D_MODEL = 1024
DEPTH = 4
POOL_WIDTH = 256
POOL_WINDOWS = (2, 4, 8, 16)
POOL_GROUP_DIM = 64
HEAD_DIM = 64
ATTN_WIDTH = 768
DILATED_PATTERNS = ((128, 1), (512, 4), (2048, 16))
ROPE_THETA = 500000.0
ROPE_DIM = 16
D_FF = 2816
IN_PROJ_WIDTH = 2560
NORM_EPS = 1e-6
MASK_VALUE = -1e30
LOG2_E = 1.4426950408889634

LANES = 128
MXU_WIDTH = 256
ROW_TILE = 512
FF_CHUNK = MXU_WIDTH
N_FF_CHUNKS = D_FF // FF_CHUNK
Q_BLOCK = 128
BAND_HALF = 64
ATTN_SLOTS = 3
MID_DIL = 4
QKV_WIDTH = 3 * ATTN_WIDTH
POOL_PAD = 16
VMEM_LIMIT = 56 * 1024 * 1024

_F32 = jnp.float32
_BF16 = jnp.bfloat16

assert all(w // (2 * d) == BAND_HALF for w, d in DILATED_PATTERNS)
assert [d for _, d in DILATED_PATTERNS] == [1, MID_DIL, MID_DIL * MID_DIL]


def _params(n_grid_dims):
    return pltpu.CompilerParams(
        dimension_semantics=("arbitrary",) * n_grid_dims,
        vmem_limit_bytes=VMEM_LIMIT)


def _resident(block_shape, index_map):
    return pl.BlockSpec(block_shape, index_map, pipeline_mode=pl.Buffered(1))


def _rmsnorm(x, g):
    y = x * lax.rsqrt(jnp.mean(x * x, axis=-1, keepdims=True) + NORM_EPS)
    return y * g


def _ffn_kernel(x_ref, g_ref, wg_ref, wu_ref, wd_ref, fg_ref, o_ref, h_ref, acc_ref, *,
                apply_final_norm):
    x = x_ref[...]
    h_ref[...] = _rmsnorm(x, g_ref[...]).astype(_BF16)

    for c in range(N_FF_CHUNKS):
        h = h_ref[...]
        gate = jnp.dot(h, wg_ref[c], preferred_element_type=_F32)
        up = jnp.dot(h, wu_ref[c], preferred_element_type=_F32)
        act = ((gate * jax.nn.sigmoid(gate)) * up).astype(_BF16)
        down = jnp.dot(act, wd_ref[c], preferred_element_type=_F32)
        if c == 0:
            acc_ref[...] = down
        elif c < N_FF_CHUNKS - 1:
            acc_ref[...] += down
    y = x + 0.5 * (acc_ref[...] + down)
    if apply_final_norm:
        y = _rmsnorm(y, fg_ref[...])
    o_ref[...] = y


def _ffn(x, gains, wg, wu, wd, final_gain, layer, apply_final_norm):
    n = x.shape[0]
    return pl.pallas_call(
        functools.partial(_ffn_kernel, apply_final_norm=apply_final_norm),
        grid=(n // ROW_TILE,),
        in_specs=[
            pl.BlockSpec((ROW_TILE, D_MODEL), lambda i: (i, 0)),
            pl.BlockSpec((None, 1, D_MODEL), lambda i: (layer, 0, 0)),
            _resident((None, N_FF_CHUNKS, D_MODEL, FF_CHUNK), lambda i: (layer, 0, 0, 0)),
            _resident((None, N_FF_CHUNKS, D_MODEL, FF_CHUNK), lambda i: (layer, 0, 0, 0)),
            _resident((None, N_FF_CHUNKS, FF_CHUNK, D_MODEL), lambda i: (layer, 0, 0, 0)),
            pl.BlockSpec((1, D_MODEL), lambda i: (0, 0)),
        ],
        out_specs=pl.BlockSpec((ROW_TILE, D_MODEL), lambda i: (i, 0)),
        out_shape=jax.ShapeDtypeStruct((n, D_MODEL), _F32),
        scratch_shapes=[pltpu.VMEM((ROW_TILE, D_MODEL), _BF16),
                        pltpu.VMEM((ROW_TILE, D_MODEL), _F32)],
        compiler_params=_params(1),
        name="ffn",
    )(x, gains, wg, wu, wd, final_gain)


def _rope_table_kernel(pos_ref, invf_ref, cos_ref, sin_lo_ref, sin_hi_ref):
    ang = pos_ref[...] * invf_ref[...]
    c = jnp.cos(ang)
    s = jnp.sin(ang)
    d = lax.broadcasted_iota(jnp.int32, ang.shape, 1) & (HEAD_DIM - 1)
    half = ROPE_DIM // 2
    cos_ref[...] = jnp.where(d < ROPE_DIM, c, 1.0)
    sin_lo_ref[...] = jnp.where(d < half, -s, 0.0)
    sin_hi_ref[...] = jnp.where((d >= half) & (d < ROPE_DIM), s, 0.0)


def _rope_tables(pos, invf):
    b, s, _ = pos.shape
    out = jax.ShapeDtypeStruct((b, s, LANES), _F32)
    spec = pl.BlockSpec((None, s, LANES), lambda i: (i, 0, 0))
    return pl.pallas_call(
        _rope_table_kernel,
        grid=(b,),
        in_specs=[pl.BlockSpec((None, s, 1), lambda i: (i, 0, 0)),
                  pl.BlockSpec((1, LANES), lambda i: (0, 0))],
        out_specs=[spec, spec, spec],
        out_shape=[out, out, out],
        compiler_params=_params(1),
        name="rope_tables",
    )(pos, invf)


def _inproj_kernel(x_ref, g_ref, w_ref, cos_ref, sin_lo_ref, sin_hi_ref,
                   vpool_ref, nat_ref, mid_ref, far_ref, slab_ref):
    h = _rmsnorm(x_ref[...], g_ref[...]).astype(_BF16)
    cos = cos_ref[...]
    sin_lo = sin_lo_ref[...]
    sin_hi = sin_hi_ref[...]
    half_rope = ROPE_DIM // 2
    n_heads_tiles = ATTN_WIDTH // LANES

    def rope(t):
        return (t * cos + pltpu.roll(t, LANES - half_rope, 1) * sin_lo
                + pltpu.roll(t, half_rope, 1) * sin_hi)

    for c in range(IN_PROJ_WIDTH // MXU_WIDTH):
        cols = slice(c * MXU_WIDTH, (c + 1) * MXU_WIDTH)
        part = jnp.dot(h, w_ref[:, cols], preferred_element_type=_F32)
        if c * MXU_WIDTH < POOL_WIDTH:
            vpool_ref[:, cols] = part
            continue
        for half in range(MXU_WIDTH // LANES):
            j = (c * MXU_WIDTH - POOL_WIDTH) // LANES + half
            t = part[:, half * LANES:(half + 1) * LANES]
            if j < n_heads_tiles:
                t = rope(t) * (LOG2_E / (HEAD_DIM ** 0.5))
            elif j < 2 * n_heads_tiles:
                t = rope(t)
            slab_ref[j] = t
            nat_ref[:, j * LANES:(j + 1) * LANES] = t.astype(_BF16)

    far_dil = MID_DIL * MID_DIL
    for j in range(QKV_WIDTH // LANES):
        lanes = slice(j * LANES, (j + 1) * LANES)
        for r in range(MID_DIL):
            rows = pl.ds(r, ROW_TILE // MID_DIL, stride=MID_DIL)
            mid_ref[r, :, lanes] = slab_ref[j, rows, :].astype(_BF16)
        for r in range(far_dil):
            rows = pl.ds(r, ROW_TILE // far_dil, stride=far_dil)
            far_ref[r, :, lanes] = slab_ref[j, rows, :].astype(_BF16)


def _inproj(x, gains, w_in, cos, sin_lo, sin_hi, layer):
    b, s, _ = x.shape
    tiles = s // ROW_TILE
    far_dil = MID_DIL * MID_DIL
    table = pl.BlockSpec((None, ROW_TILE, LANES), lambda i, t: (i, t, 0))
    return pl.pallas_call(
        _inproj_kernel,
        grid=(b, tiles),
        in_specs=[
            pl.BlockSpec((None, ROW_TILE, D_MODEL), lambda i, t: (i, t, 0)),
            pl.BlockSpec((None, 1, D_MODEL), lambda i, t: (layer, 0, 0)),
            _resident((None, D_MODEL, IN_PROJ_WIDTH), lambda i, t: (layer, 0, 0)),
            table, table, table,
        ],
        out_specs=[
            pl.BlockSpec((None, ROW_TILE, POOL_WIDTH), lambda i, t: (i, t, 0)),
            pl.BlockSpec((None, ROW_TILE, QKV_WIDTH), lambda i, t: (i, t, 0)),
            pl.BlockSpec((None, MID_DIL, ROW_TILE // MID_DIL, QKV_WIDTH),
                         lambda i, t: (i, 0, t, 0)),
            pl.BlockSpec((None, far_dil, ROW_TILE // far_dil, QKV_WIDTH),
                         lambda i, t: (i, 0, t, 0)),
        ],
        out_shape=[
            jax.ShapeDtypeStruct((b, s, POOL_WIDTH), _F32),
            jax.ShapeDtypeStruct((b, s, QKV_WIDTH), _BF16),
            jax.ShapeDtypeStruct((b, MID_DIL, s // MID_DIL, QKV_WIDTH), _BF16),
            jax.ShapeDtypeStruct((b, far_dil, s // far_dil, QKV_WIDTH), _BF16),
        ],
        scratch_shapes=[pltpu.VMEM((QKV_WIDTH // LANES, ROW_TILE, LANES), _F32)],
        compiler_params=_params(2),
        name="inproj",
    )(x, gains, w_in, cos, sin_lo, sin_hi)


def _pool_kernel(v_ref, w_ref, scale_ref, o_ref, pad_ref):
    s = v_ref.shape[0]
    tile = pl.program_id(1)
    v = v_ref[...]
    pad_ref[...] = jnp.zeros_like(pad_ref)
    pad_ref[POOL_PAD:POOL_PAD + s, :] = v
    x = pad_ref[...]
    rows = x.shape[0]

    def shifted(t, k):
        return pltpu.roll(t, k % rows, 0)

    w2 = x + shifted(x, 1)
    w4 = shifted(w2, 1) + shifted(w2, -1)
    w8 = shifted(w4, 2) + shifted(w4, -2)
    w16 = shifted(w8, 4) + shifted(w8, -4)
    first_group = lax.broadcasted_iota(jnp.int32, x.shape, 1) < POOL_GROUP_DIM
    small = jnp.where(first_group, w2, w4)
    large = jnp.where(first_group, w8, w16)
    wsum = jnp.where(tile == 0, small, large)[POOL_PAD:POOL_PAD + s, :]

    lane_first = lax.broadcasted_iota(jnp.int32, v.shape, 1) < POOL_GROUP_DIM
    half_w = jnp.where(tile == 0, jnp.where(lane_first, 1, 2), jnp.where(lane_first, 4, 8))
    pos = lax.broadcasted_iota(jnp.int32, v.shape, 0)
    lo = jnp.maximum(pos - half_w, 0)
    hi = jnp.minimum(pos + half_w - 1, s - 1)
    cnt = (hi - lo + 1).astype(_F32)
    diff = wsum / cnt - v
    y = jnp.dot(diff.astype(_BF16), w_ref[...], preferred_element_type=_F32)
    o_ref[...] = (y * scale_ref[...]).astype(o_ref.dtype)


def _pool(v_pool, pool_w_bd, pool_scale, layer):
    b, s, _ = v_pool.shape
    n_tiles = POOL_WIDTH // LANES
    return pl.pallas_call(
        _pool_kernel,
        grid=(b, n_tiles),
        in_specs=[
            pl.BlockSpec((None, s, LANES), lambda i, j: (i, 0, j)),
            pl.BlockSpec((None, None, LANES, LANES), lambda i, j: (layer, j, 0, 0)),
            pl.BlockSpec((None, 1, LANES), lambda i, j: (layer, 0, j)),
        ],
        out_specs=pl.BlockSpec((None, s, LANES), lambda i, j: (i, 0, j)),
        out_shape=jax.ShapeDtypeStruct((b, s, POOL_WIDTH), _BF16),
        scratch_shapes=[pltpu.VMEM((s + 2 * POOL_PAD, LANES), _F32)],
        compiler_params=_params(2),
        name="pool",
    )(v_pool, pool_w_bd, pool_scale)


def _attn_kernel(qn_ref, kn_ref, vn_ref, qm_ref, km_ref, vm_ref, qf_ref, kf_ref, vf_ref, o_ref,
                 mask_ref, s_ref, p_ref, st_ref, num_far, m_far, l_far, num_mid, m_mid, l_mid):
    s_len = qn_ref.shape[1]
    far_dil = MID_DIL * MID_DIL
    operands = ((qn_ref, kn_ref, vn_ref), (qm_ref, km_ref, vm_ref), (qf_ref, kf_ref, vf_ref))
    first = lax.broadcasted_iota(jnp.int32, (Q_BLOCK, LANES), 1) < HEAD_DIM

    rr = lax.broadcasted_iota(jnp.int32, (Q_BLOCK, 2 * Q_BLOCK), 0)
    cc = lax.broadcasted_iota(jnp.int32, (Q_BLOCK, 2 * Q_BLOCK), 1)
    for idx in range(3):
        off = (idx - 2) * BAND_HALF
        mask_ref[idx] = (jnp.abs(cc - rr + off) <= BAND_HALF).astype(_F32)

    blocks = []
    for order, n_seq in ((2, far_dil), (1, MID_DIL), (0, 1)):
        seq = s_len // n_seq
        blocks += [(order, r, q0, seq) for r in range(n_seq) for q0 in range(0, seq, Q_BLOCK)]

    def key_window(q0, seq):
        kwin = min(2 * Q_BLOCK, seq)
        return min(max(q0 - BAND_HALF, 0), seq - kwin), kwin

    def scores(blk, slot):
        order, r, q0, seq = blk
        q_ref, k_ref, _ = operands[order]
        k0, kwin = key_window(q0, seq)
        q = q_ref[r, q0:q0 + Q_BLOCK, :]
        kblk = k_ref[r, k0:k0 + kwin, :]
        valid = mask_ref[(k0 - q0) // BAND_HALF + 2, :, 0:kwin] != 0.0
        zero = jnp.zeros_like(q)
        for h, qh in enumerate((jnp.where(first, q, zero), jnp.where(first, zero, q))):
            sc = lax.dot_general(qh, kblk, (((1,), (1,)), ((), ())),
                                 preferred_element_type=_F32)
            s_ref[slot, h, :, 0:kwin] = jnp.where(valid, sc, MASK_VALUE)

    def softmax(blk, slot):
        _, kwin = key_window(blk[2], blk[3])
        stats = []
        for h in range(2):
            sc = s_ref[slot, h, :, 0:kwin]
            m = jnp.max(sc, axis=-1, keepdims=True)
            p = jnp.exp2(sc - m)
            stats.append((m, jnp.sum(p, axis=-1, keepdims=True)))
            p_ref[slot, h, :, 0:kwin] = p.astype(_BF16)
        st_ref[slot, 0] = jnp.where(first, stats[0][0], stats[1][0])
        st_ref[slot, 1] = jnp.where(first, stats[0][1], stats[1][1])

    def values(blk, slot):
        order, r, q0, seq = blk
        v_ref = operands[order][2]
        k0, kwin = key_window(q0, seq)
        vblk = v_ref[r, k0:k0 + kwin, :]
        o = [jnp.dot(p_ref[slot, h, :, 0:kwin], vblk, preferred_element_type=_F32)
             for h in range(2)]
        num = jnp.where(first, o[0], o[1])
        m = st_ref[slot, 0]
        l = st_ref[slot, 1]
        rows = slice(q0, q0 + Q_BLOCK)
        if order == 2:
            out_rows = pl.ds(MID_DIL * q0 + r // MID_DIL, Q_BLOCK, stride=MID_DIL)
            num_far[r % MID_DIL, out_rows, :] = num
            m_far[r % MID_DIL, out_rows, :] = m
            l_far[r % MID_DIL, out_rows, :] = l
            return
        if order == 1:
            m_c, num_c, l_c = m_far[r, rows, :], num_far[r, rows, :], l_far[r, rows, :]
        else:
            m_c, num_c, l_c = m_mid[rows, :], num_mid[rows, :], l_mid[rows, :]
        m_new = jnp.maximum(m, m_c)
        a = jnp.exp2(m - m_new)
        a_c = jnp.exp2(m_c - m_new)
        num = a * num + a_c * num_c
        l = a * l + a_c * l_c
        if order == 1:
            out_rows = pl.ds(MID_DIL * q0 + r, Q_BLOCK, stride=MID_DIL)
            num_mid[out_rows, :] = num
            m_mid[out_rows, :] = m_new
            l_mid[out_rows, :] = l
        else:
            o_ref[rows, :] = (num / l).astype(o_ref.dtype)

    stages = (scores, softmax, values)
    for step in range(len(blocks) + len(stages) - 1):
        for depth, stage in enumerate(stages):
            i = step - depth
            if 0 <= i < len(blocks):
                stage(blocks[i], i % ATTN_SLOTS)


def _attention(nat, mid, far):
    b, s, _ = nat.shape
    n_pairs = ATTN_WIDTH // LANES
    far_dil = MID_DIL * MID_DIL

    def specs(n_seq):
        def tile(off):
            return pl.BlockSpec((None, n_seq, s // n_seq, LANES),
                                lambda i, j: (i, 0, 0, off + j))
        return [tile(0), tile(n_pairs), tile(2 * n_pairs)]

    return pl.pallas_call(
        _attn_kernel,
        grid=(b, n_pairs),
        in_specs=specs(1) + specs(MID_DIL) + specs(far_dil),
        out_specs=pl.BlockSpec((None, s, LANES), lambda i, j: (i, 0, j)),
        out_shape=jax.ShapeDtypeStruct((b, s, ATTN_WIDTH), _BF16),
        scratch_shapes=[pltpu.VMEM((3, Q_BLOCK, 2 * Q_BLOCK), _F32),
                        pltpu.VMEM((ATTN_SLOTS, 2, Q_BLOCK, 2 * Q_BLOCK), _F32),
                        pltpu.VMEM((ATTN_SLOTS, 2, Q_BLOCK, 2 * Q_BLOCK), _BF16),
                        pltpu.VMEM((ATTN_SLOTS, 2, Q_BLOCK, LANES), _F32)]
        + [pltpu.VMEM((MID_DIL, s // MID_DIL, LANES), _F32)] * 3
        + [pltpu.VMEM((s, LANES), _F32)] * 3,
        compiler_params=_params(2),
        name="dilated_attention",
    )(nat.reshape(b, 1, s, QKV_WIDTH), nat.reshape(b, 1, s, QKV_WIDTH),
      nat.reshape(b, 1, s, QKV_WIDTH), mid, mid, mid, far, far, far)


def _outproj_kernel(x_ref, yp_ref, ya_ref, w_ref, o_ref):
    acc = jnp.dot(yp_ref[...], w_ref[0:POOL_WIDTH, :], preferred_element_type=_F32)
    acc = acc + jnp.dot(ya_ref[...], w_ref[POOL_WIDTH:D_MODEL, :], preferred_element_type=_F32)
    o_ref[...] = x_ref[...] + acc


def _outproj(x, y_pool, y_attn, w_out, layer):
    n = x.shape[0]
    return pl.pallas_call(
        _outproj_kernel,
        grid=(n // ROW_TILE,),
        in_specs=[
            pl.BlockSpec((ROW_TILE, D_MODEL), lambda i: (i, 0)),
            pl.BlockSpec((ROW_TILE, POOL_WIDTH), lambda i: (i, 0)),
            pl.BlockSpec((ROW_TILE, ATTN_WIDTH), lambda i: (i, 0)),
            _resident((None, D_MODEL, D_MODEL), lambda i: (layer, 0, 0)),
        ],
        out_specs=pl.BlockSpec((ROW_TILE, D_MODEL), lambda i: (i, 0)),
        out_shape=jax.ShapeDtypeStruct((n, D_MODEL), _F32),
        compiler_params=_params(1),
        name="outproj",
    )(x, y_pool, y_attn, w_out)


def _chunk_columns(w):
    w = w.reshape(DEPTH, D_MODEL, N_FF_CHUNKS, FF_CHUNK)
    return jnp.transpose(w, (0, 2, 1, 3)).astype(_BF16)


def _chunk_rows(w):
    return w.reshape(DEPTH, N_FF_CHUNKS, FF_CHUNK, D_MODEL).astype(_BF16)


def _pool_block_diag(pool_w):
    g = POOL_GROUP_DIM
    tiles = POOL_WIDTH // LANES
    per_tile = LANES // g
    z = jnp.zeros((DEPTH, tiles, LANES, LANES), pool_w.dtype)
    for t in range(tiles):
        for h in range(per_tile):
            z = z.at[:, t, h * g:(h + 1) * g, h * g:(h + 1) * g].set(pool_w[:, per_tile * t + h])
    return z.astype(_BF16)


@jax.jit
def _forward(x, positions, ffn1_norm, ffn1_w_gate, ffn1_w_up, ffn1_w_down, mix_norm, w_in,
             pool_w, pool_scale, w_out, ffn2_norm, ffn2_w_gate, ffn2_w_up, ffn2_w_down,
             final_norm):
    b, s, d = x.shape
    n = b * s
    assert d == D_MODEL and s % ROW_TILE == 0 and s % (Q_BLOCK * MID_DIL * MID_DIL) == 0
    assert all(w // 2 == POOL_WINDOWS[0] * 2 ** i // 2 for i, w in enumerate(POOL_WINDOWS))

    wg1, wu1, wd1 = _chunk_columns(ffn1_w_gate), _chunk_columns(ffn1_w_up), _chunk_rows(ffn1_w_down)
    wg2, wu2, wd2 = _chunk_columns(ffn2_w_gate), _chunk_columns(ffn2_w_up), _chunk_rows(ffn2_w_down)
    w_in_b = w_in.astype(_BF16)
    w_out_b = w_out.astype(_BF16)
    pool_w_bd = _pool_block_diag(pool_w)
    g1 = ffn1_norm.reshape(DEPTH, 1, D_MODEL)
    g2 = ffn2_norm.reshape(DEPTH, 1, D_MODEL)
    gm = mix_norm.reshape(DEPTH, 1, D_MODEL)
    ps = pool_scale.reshape(DEPTH, 1, POOL_WIDTH)
    gf = final_norm.reshape(1, D_MODEL)

    inv_freq = ROPE_THETA ** (-jnp.arange(0, ROPE_DIM, 2, dtype=_F32) / ROPE_DIM)
    invf = jnp.tile(inv_freq, LANES // (ROPE_DIM // 2)).reshape(1, LANES)
    pos = positions.astype(_F32)[..., None]
    cos, sin_lo, sin_hi = _rope_tables(pos, invf)

    xs = x.reshape(n, D_MODEL)
    for layer in range(DEPTH):
        xs = _ffn(xs, g1, wg1, wu1, wd1, gf, layer, False)
        v_pool, nat, mid, far = _inproj(xs.reshape(b, s, D_MODEL), gm, w_in_b, cos, sin_lo,
                                        sin_hi, layer)
        y_pool = _pool(v_pool, pool_w_bd, ps, layer)
        y_attn = _attention(nat, mid, far)
        xs = _outproj(xs, y_pool.reshape(n, POOL_WIDTH), y_attn.reshape(n, ATTN_WIDTH),
                      w_out_b, layer)
        xs = _ffn(xs, g2, wg2, wu2, wd2, gf, layer, layer == DEPTH - 1)
    return xs.reshape(b, s, D_MODEL)


def kernel(x, positions, ffn1_norm, ffn1_w_gate, ffn1_w_up, ffn1_w_down, mix_norm, w_in,
           pool_w, pool_scale, w_out, ffn2_norm, ffn2_w_gate, ffn2_w_up, ffn2_w_down,
           final_norm):
    return _forward(x, positions, ffn1_norm, ffn1_w_gate, ffn1_w_up, ffn1_w_down, mix_norm,
                    w_in, pool_w, pool_scale, w_out, ffn2_norm, ffn2_w_gate, ffn2_w_up,
                    ffn2_w_down, final_norm)
```

```python
import functools

import jax
import jax.numpy as jnp
from jax import lax
from jax.experimental import pallas as pl
from jax.experimental.pallas import tpu as pltpu

D_MODEL = 1024
DEPTH = 4
POOL_WIDTH = 256
POOL_WINDOWS = (2, 4, 8, 16)
POOL_GROUP_DIM = 64
HEAD_DIM = 64
ATTN_WIDTH = 768
DILATED_PATTERNS = ((128, 1), (512, 4), (2048, 16))
ROPE_THETA = 500000.0
ROPE_DIM = 16
D_FF = 2816
IN_PROJ_WIDTH = 2560
NORM_EPS = 1e-6
MASK_VALUE = -1e30
LOG2_E = 1.4426950408889634

LANES = 128
MXU_WIDTH = 256
ROW_TILE = 512
FF_CHUNK = MXU_WIDTH
N_FF_CHUNKS = D_FF // FF_CHUNK
Q_BLOCK = 128
BAND_HALF = 64
ATTN_SLOTS = 3
MID_DIL = 4
QKV_WIDTH = 3 * ATTN_WIDTH
POOL_PAD = 16
VMEM_LIMIT = 56 * 1024 * 1024

_F32 = jnp.float32
_BF16 = jnp.bfloat16

assert all(w // (2 * d) == BAND_HALF for w, d in DILATED_PATTERNS)
assert [d for _, d in DILATED_PATTERNS] == [1, MID_DIL, MID_DIL * MID_DIL]


def _params(n_grid_dims):
    return pltpu.CompilerParams(
        dimension_semantics=("arbitrary",) * n_grid_dims,
        vmem_limit_bytes=VMEM_LIMIT)


def _resident(block_shape, index_map):
    return pl.BlockSpec(block_shape, index_map, pipeline_mode=pl.Buffered(1))


def _rmsnorm(x, g):
    y = x * lax.rsqrt(jnp.mean(x * x, axis=-1, keepdims=True) + NORM_EPS)
    return y * g


def _ffn_kernel(x_ref, g_ref, wg_ref, wu_ref, wd_ref, fg_ref, o_ref, h_ref, acc_ref, *,
                apply_final_norm):
    x = x_ref[...]
    h_ref[...] = _rmsnorm(x, g_ref[...]).astype(_BF16)

    for c in range(N_FF_CHUNKS):
        cols = slice(c * FF_CHUNK, (c + 1) * FF_CHUNK)
        h = h_ref[...]
        gate = jnp.dot(h, wg_ref[:, cols], preferred_element_type=_F32)
        up = jnp.dot(h, wu_ref[:, cols], preferred_element_type=_F32)
        act = ((gate * jax.nn.sigmoid(gate)) * up).astype(_BF16)
        down = jnp.dot(act, wd_ref[cols, :], preferred_element_type=_F32)
        if c == 0:
            acc_ref[...] = down
        elif c < N_FF_CHUNKS - 1:
            acc_ref[...] += down
    y = x + 0.5 * (acc_ref[...] + down)
    if apply_final_norm:
        y = _rmsnorm(y, fg_ref[...])
    o_ref[...] = y


def _ffn(x, gains, wg, wu, wd, final_gain, layer, apply_final_norm):
    n = x.shape[0]
    return pl.pallas_call(
        functools.partial(_ffn_kernel, apply_final_norm=apply_final_norm),
        grid=(n // ROW_TILE,),
        in_specs=[
            pl.BlockSpec((ROW_TILE, D_MODEL), lambda i: (i, 0)),
            pl.BlockSpec((None, 1, D_MODEL), lambda i: (layer, 0, 0)),
            _resident((None, D_MODEL, D_FF), lambda i: (layer, 0, 0)),
            _resident((None, D_MODEL, D_FF), lambda i: (layer, 0, 0)),
            _resident((None, D_FF, D_MODEL), lambda i: (layer, 0, 0)),
            pl.BlockSpec((1, D_MODEL), lambda i: (0, 0)),
        ],
        out_specs=pl.BlockSpec((ROW_TILE, D_MODEL), lambda i: (i, 0)),
        out_shape=jax.ShapeDtypeStruct((n, D_MODEL), _F32),
        scratch_shapes=[pltpu.VMEM((ROW_TILE, D_MODEL), _BF16),
                        pltpu.VMEM((ROW_TILE, D_MODEL), _F32)],
        compiler_params=_params(1),
        name="ffn",
    )(x, gains, wg, wu, wd, final_gain)


def _rope_table_kernel(pos_ref, invf_ref, cos_ref, sin_lo_ref, sin_hi_ref):
    ang = pos_ref[...] * invf_ref[...]
    c = jnp.cos(ang)
    s = jnp.sin(ang)
    d = lax.broadcasted_iota(jnp.int32, ang.shape, 1) & (HEAD_DIM - 1)
    half = ROPE_DIM // 2
    cos_ref[...] = jnp.where(d < ROPE_DIM, c, 1.0)
    sin_lo_ref[...] = jnp.where(d < half, -s, 0.0)
    sin_hi_ref[...] = jnp.where((d >= half) & (d < ROPE_DIM), s, 0.0)


def _rope_tables(pos, invf):
    b, s, _ = pos.shape
    out = jax.ShapeDtypeStruct((b, s, LANES), _F32)
    spec = pl.BlockSpec((None, s, LANES), lambda i: (i, 0, 0))
    return pl.pallas_call(
        _rope_table_kernel,
        grid=(b,),
        in_specs=[pl.BlockSpec((None, s, 1), lambda i: (i, 0, 0)),
                  pl.BlockSpec((1, LANES), lambda i: (0, 0))],
        out_specs=[spec, spec, spec],
        out_shape=[out, out, out],
        compiler_params=_params(1),
        name="rope_tables",
    )(pos, invf)


def _inproj_kernel(x_ref, g_ref, w_ref, cos_ref, sin_lo_ref, sin_hi_ref,
                   vpool_ref, nat_ref, mid_ref, far_ref, slab_ref, midslab_ref):
    h = _rmsnorm(x_ref[...], g_ref[...]).astype(_BF16)
    cos = cos_ref[...]
    sin_lo = sin_lo_ref[...]
    sin_hi = sin_hi_ref[...]
    half_rope = ROPE_DIM // 2
    n_heads_tiles = ATTN_WIDTH // LANES

    def rope(t):
        return (t * cos + pltpu.roll(t, LANES - half_rope, 1) * sin_lo
                + pltpu.roll(t, half_rope, 1) * sin_hi)

    for c in range(IN_PROJ_WIDTH // MXU_WIDTH):
        cols = slice(c * MXU_WIDTH, (c + 1) * MXU_WIDTH)
        part = jnp.dot(h, w_ref[:, cols], preferred_element_type=_F32)
        if c * MXU_WIDTH < POOL_WIDTH:
            vpool_ref[:, cols] = part
            continue
        for half in range(MXU_WIDTH // LANES):
            j = (c * MXU_WIDTH - POOL_WIDTH) // LANES + half
            t = part[:, half * LANES:(half + 1) * LANES]
            if j < n_heads_tiles:
                t = rope(t) * (LOG2_E / (HEAD_DIM ** 0.5))
            elif j < 2 * n_heads_tiles:
                t = rope(t)
            slab_ref[j] = t
            nat_ref[:, j * LANES:(j + 1) * LANES] = t.astype(_BF16)

    far_dil = MID_DIL * MID_DIL
    for j in range(QKV_WIDTH // LANES):
        lanes = slice(j * LANES, (j + 1) * LANES)
        for r in range(MID_DIL):
            rows = pl.ds(r, ROW_TILE // MID_DIL, stride=MID_DIL)
            picked = slab_ref[j, rows, :]
            mid_ref[r, :, lanes] = picked.astype(_BF16)
            midslab_ref[j, r] = picked
        for r in range(far_dil):
            rows = pl.ds(r // MID_DIL, ROW_TILE // far_dil, stride=MID_DIL)
            far_ref[r, :, lanes] = midslab_ref[j, r % MID_DIL, rows, :].astype(_BF16)


def _inproj(x, gains, w_in, cos, sin_lo, sin_hi, layer):
    b, s, _ = x.shape
    tiles = s // ROW_TILE
    far_dil = MID_DIL * MID_DIL
    table = pl.BlockSpec((None, ROW_TILE, LANES), lambda i, t: (i, t, 0))
    return pl.pallas_call(
        _inproj_kernel,
        grid=(b, tiles),
        in_specs=[
            pl.BlockSpec((None, ROW_TILE, D_MODEL), lambda i, t: (i, t, 0)),
            pl.BlockSpec((None, 1, D_MODEL), lambda i, t: (layer, 0, 0)),
            _resident((None, D_MODEL, IN_PROJ_WIDTH), lambda i, t: (layer, 0, 0)),
            table, table, table,
        ],
        out_specs=[
            pl.BlockSpec((None, ROW_TILE, POOL_WIDTH), lambda i, t: (i, t, 0)),
            pl.BlockSpec((None, ROW_TILE, QKV_WIDTH), lambda i, t: (i, t, 0)),
            pl.BlockSpec((None, MID_DIL, ROW_TILE // MID_DIL, QKV_WIDTH),
                         lambda i, t: (i, 0, t, 0)),
            pl.BlockSpec((None, far_dil, ROW_TILE // far_dil, QKV_WIDTH),
                         lambda i, t: (i, 0, t, 0)),
        ],
        out_shape=[
            jax.ShapeDtypeStruct((b, s, POOL_WIDTH), _F32),
            jax.ShapeDtypeStruct((b, s, QKV_WIDTH), _BF16),
            jax.ShapeDtypeStruct((b, MID_DIL, s // MID_DIL, QKV_WIDTH), _BF16),
            jax.ShapeDtypeStruct((b, far_dil, s // far_dil, QKV_WIDTH), _BF16),
        ],
        scratch_shapes=[
            pltpu.VMEM((QKV_WIDTH // LANES, ROW_TILE, LANES), _F32),
            pltpu.VMEM((QKV_WIDTH // LANES, MID_DIL, ROW_TILE // MID_DIL, LANES), _F32)],
        compiler_params=_params(2),
        name="inproj",
    )(x, gains, w_in, cos, sin_lo, sin_hi)


def _pool_kernel(v_ref, w_ref, scale_ref, o_ref, pad_ref):
    s = v_ref.shape[0]
    tile = pl.program_id(1)
    v = v_ref[...]
    pad_ref[...] = jnp.zeros_like(pad_ref)
    pad_ref[POOL_PAD:POOL_PAD + s, :] = v
    x = pad_ref[...]
    rows = x.shape[0]

    def shifted(t, k):
        return pltpu.roll(t, k % rows, 0)

    w2 = x + shifted(x, 1)
    w4 = shifted(w2, 1) + shifted(w2, -1)
    w8 = shifted(w4, 2) + shifted(w4, -2)
    w16 = shifted(w8, 4) + shifted(w8, -4)
    first_group = lax.broadcasted_iota(jnp.int32, x.shape, 1) < POOL_GROUP_DIM
    small = jnp.where(first_group, w2, w4)
    large = jnp.where(first_group, w8, w16)
    wsum = jnp.where(tile == 0, small, large)[POOL_PAD:POOL_PAD + s, :]

    lane_first = lax.broadcasted_iota(jnp.int32, v.shape, 1) < POOL_GROUP_DIM
    half_w = jnp.where(tile == 0, jnp.where(lane_first, 1, 2), jnp.where(lane_first, 4, 8))
    pos = lax.broadcasted_iota(jnp.int32, v.shape, 0)
    lo = jnp.maximum(pos - half_w, 0)
    hi = jnp.minimum(pos + half_w - 1, s - 1)
    cnt = (hi - lo + 1).astype(_F32)
    diff = wsum / cnt - v
    y = jnp.dot(diff.astype(_BF16), w_ref[...], preferred_element_type=_F32)
    o_ref[...] = (y * scale_ref[...]).astype(o_ref.dtype)


def _pool(v_pool, pool_w_bd, pool_scale, layer):
    b, s, _ = v_pool.shape
    n_tiles = POOL_WIDTH // LANES
    return pl.pallas_call(
        _pool_kernel,
        grid=(b, n_tiles),
        in_specs=[
            pl.BlockSpec((None, s, LANES), lambda i, j: (i, 0, j)),
            pl.BlockSpec((None, None, LANES, LANES), lambda i, j: (layer, j, 0, 0)),
            pl.BlockSpec((None, 1, LANES), lambda i, j: (layer, 0, j)),
        ],
        out_specs=pl.BlockSpec((None, s, LANES), lambda i, j: (i, 0, j)),
        out_shape=jax.ShapeDtypeStruct((b, s, POOL_WIDTH), _BF16),
        scratch_shapes=[pltpu.VMEM((s + 2 * POOL_PAD, LANES), _F32)],
        compiler_params=_params(2),
        name="pool",
    )(v_pool, pool_w_bd, pool_scale)


def _attn_kernel(qn_ref, kn_ref, vn_ref, qm_ref, km_ref, vm_ref, qf_ref, kf_ref, vf_ref, o_ref,
                 mask_ref, s_ref, p_ref, st_ref, num_far, m_far, l_far, num_mid, m_mid, l_mid):
    s_len = qn_ref.shape[1]
    far_dil = MID_DIL * MID_DIL
    operands = ((qn_ref, kn_ref, vn_ref), (qm_ref, km_ref, vm_ref), (qf_ref, kf_ref, vf_ref))
    first = lax.broadcasted_iota(jnp.int32, (Q_BLOCK, LANES), 1) < HEAD_DIM

    rr = lax.broadcasted_iota(jnp.int32, (Q_BLOCK, 2 * Q_BLOCK), 0)
    cc = lax.broadcasted_iota(jnp.int32, (Q_BLOCK, 2 * Q_BLOCK), 1)
    for idx in range(3):
        off = (idx - 2) * BAND_HALF
        mask_ref[idx] = (jnp.abs(cc - rr + off) <= BAND_HALF).astype(_F32)

    blocks = []
    for order, n_seq in ((2, far_dil), (1, MID_DIL), (0, 1)):
        seq = s_len // n_seq
        blocks += [(order, r, q0, seq) for r in range(n_seq) for q0 in range(0, seq, Q_BLOCK)]

    def key_window(q0, seq):
        kwin = min(2 * Q_BLOCK, seq)
        return min(max(q0 - BAND_HALF, 0), seq - kwin), kwin

    def scores(blk, slot):
        order, r, q0, seq = blk
        q_ref, k_ref, _ = operands[order]
        k0, kwin = key_window(q0, seq)
        q = q_ref[r, q0:q0 + Q_BLOCK, :]
        kblk = k_ref[r, k0:k0 + kwin, :]
        valid = mask_ref[(k0 - q0) // BAND_HALF + 2, :, 0:kwin] != 0.0
        zero = jnp.zeros_like(q)
        for h, qh in enumerate((jnp.where(first, q, zero), jnp.where(first, zero, q))):
            sc = lax.dot_general(qh, kblk, (((1,), (1,)), ((), ())),
                                 preferred_element_type=_F32)
            s_ref[slot, h, :, 0:kwin] = jnp.where(valid, sc, MASK_VALUE)

    def softmax(blk, slot):
        _, kwin = key_window(blk[2], blk[3])
        stats = []
        for h in range(2):
            sc = s_ref[slot, h, :, 0:kwin]
            m = jnp.max(sc, axis=-1, keepdims=True)
            p = jnp.exp2(sc - m)
            stats.append((m, jnp.sum(p, axis=-1, keepdims=True)))
            p_ref[slot, h, :, 0:kwin] = p.astype(_BF16)
        st_ref[slot, 0] = jnp.where(first, stats[0][0], stats[1][0])
        st_ref[slot, 1] = jnp.where(first, stats[0][1], stats[1][1])

    def values(blk, slot):
        order, r, q0, seq = blk
        v_ref = operands[order][2]
        k0, kwin = key_window(q0, seq)
        vblk = v_ref[r, k0:k0 + kwin, :]
        o = [jnp.dot(p_ref[slot, h, :, 0:kwin], vblk, preferred_element_type=_F32)
             for h in range(2)]
        num = jnp.where(first, o[0], o[1])
        m = st_ref[slot, 0]
        l = st_ref[slot, 1]
        rows = slice(q0, q0 + Q_BLOCK)
        if order == 2:
            out_rows = pl.ds(MID_DIL * q0 + r // MID_DIL, Q_BLOCK, stride=MID_DIL)
            num_far[r % MID_DIL, out_rows, :] = num
            m_far[r % MID_DIL, out_rows, :] = m
            l_far[r % MID_DIL, out_rows, :] = l
            return
        if order == 1:
            m_c, num_c, l_c = m_far[r, rows, :], num_far[r, rows, :], l_far[r, rows, :]
        else:
            m_c, num_c, l_c = m_mid[rows, :], num_mid[rows, :], l_mid[rows, :]
        m_new = jnp.maximum(m, m_c)
        a = jnp.exp2(m - m_new)
        a_c = jnp.exp2(m_c - m_new)
        num = a * num + a_c * num_c
        l = a * l + a_c * l_c
        if order == 1:
            out_rows = pl.ds(MID_DIL * q0 + r, Q_BLOCK, stride=MID_DIL)
            num_mid[out_rows, :] = num
            m_mid[out_rows, :] = m_new
            l_mid[out_rows, :] = l
        else:
            o_ref[rows, :] = (num / l).astype(o_ref.dtype)

    stages = (scores, softmax, values)
    for step in range(len(blocks) + len(stages) - 1):
        for depth, stage in enumerate(stages):
            i = step - depth
            if 0 <= i < len(blocks):
                stage(blocks[i], i % ATTN_SLOTS)


def _attention(nat, mid, far):
    b, s, _ = nat.shape
    n_pairs = ATTN_WIDTH // LANES
    far_dil = MID_DIL * MID_DIL

    def specs(n_seq):
        def tile(off):
            return pl.BlockSpec((None, n_seq, s // n_seq, LANES),
                                lambda i, j: (i, 0, 0, off + j))
        return [tile(0), tile(n_pairs), tile(2 * n_pairs)]

    return pl.pallas_call(
        _attn_kernel,
        grid=(b, n_pairs),
        in_specs=specs(1) + specs(MID_DIL) + specs(far_dil),
        out_specs=pl.BlockSpec((None, s, LANES), lambda i, j: (i, 0, j)),
        out_shape=jax.ShapeDtypeStruct((b, s, ATTN_WIDTH), _BF16),
        scratch_shapes=[pltpu.VMEM((3, Q_BLOCK, 2 * Q_BLOCK), _F32),
                        pltpu.VMEM((ATTN_SLOTS, 2, Q_BLOCK, 2 * Q_BLOCK), _F32),
                        pltpu.VMEM((ATTN_SLOTS, 2, Q_BLOCK, 2 * Q_BLOCK), _BF16),
                        pltpu.VMEM((ATTN_SLOTS, 2, Q_BLOCK, LANES), _F32)]
        + [pltpu.VMEM((MID_DIL, s // MID_DIL, LANES), _F32)] * 3
        + [pltpu.VMEM((s, LANES), _F32)] * 3,
        compiler_params=_params(2),
        name="dilated_attention",
    )(nat.reshape(b, 1, s, QKV_WIDTH), nat.reshape(b, 1, s, QKV_WIDTH),
      nat.reshape(b, 1, s, QKV_WIDTH), mid, mid, mid, far, far, far)


def _outproj_kernel(x_ref, yp_ref, ya_ref, w_ref, o_ref):
    acc = jnp.dot(yp_ref[...], w_ref[0:POOL_WIDTH, :], preferred_element_type=_F32)
    acc = acc + jnp.dot(ya_ref[...], w_ref[POOL_WIDTH:D_MODEL, :], preferred_element_type=_F32)
    o_ref[...] = x_ref[...] + acc


def _outproj(x, y_pool, y_attn, w_out, layer):
    n = x.shape[0]
    return pl.pallas_call(
        _outproj_kernel,
        grid=(n // ROW_TILE,),
        in_specs=[
            pl.BlockSpec((ROW_TILE, D_MODEL), lambda i: (i, 0)),
            pl.BlockSpec((ROW_TILE, POOL_WIDTH), lambda i: (i, 0)),
            pl.BlockSpec((ROW_TILE, ATTN_WIDTH), lambda i: (i, 0)),
            _resident((None, D_MODEL, D_MODEL), lambda i: (layer, 0, 0)),
        ],
        out_specs=pl.BlockSpec((ROW_TILE, D_MODEL), lambda i: (i, 0)),
        out_shape=jax.ShapeDtypeStruct((n, D_MODEL), _F32),
        compiler_params=_params(1),
        name="outproj",
    )(x, y_pool, y_attn, w_out)


def _pool_block_diag(pool_w):
    g = POOL_GROUP_DIM
    tiles = POOL_WIDTH // LANES
    per_tile = LANES // g
    z = jnp.zeros((DEPTH, tiles, LANES, LANES), pool_w.dtype)
    for t in range(tiles):
        for h in range(per_tile):
            z = z.at[:, t, h * g:(h + 1) * g, h * g:(h + 1) * g].set(pool_w[:, per_tile * t + h])
    return z.astype(_BF16)


@jax.jit
def _forward(x, positions, ffn1_norm, ffn1_w_gate, ffn1_w_up, ffn1_w_down, mix_norm, w_in,
             pool_w, pool_scale, w_out, ffn2_norm, ffn2_w_gate, ffn2_w_up, ffn2_w_down,
             final_norm):
    b, s, d = x.shape
    n = b * s
    assert d == D_MODEL and s % ROW_TILE == 0 and s % (Q_BLOCK * MID_DIL * MID_DIL) == 0
    assert all(w // 2 == POOL_WINDOWS[0] * 2 ** i // 2 for i, w in enumerate(POOL_WINDOWS))

    wg1, wu1, wd1 = (w.astype(_BF16) for w in (ffn1_w_gate, ffn1_w_up, ffn1_w_down))
    wg2, wu2, wd2 = (w.astype(_BF16) for w in (ffn2_w_gate, ffn2_w_up, ffn2_w_down))
    w_in_b = w_in.astype(_BF16)
    w_out_b = w_out.astype(_BF16)
    pool_w_bd = _pool_block_diag(pool_w)
    g1 = ffn1_norm.reshape(DEPTH, 1, D_MODEL)
    g2 = ffn2_norm.reshape(DEPTH, 1, D_MODEL)
    gm = mix_norm.reshape(DEPTH, 1, D_MODEL)
    ps = pool_scale.reshape(DEPTH, 1, POOL_WIDTH)
    gf = final_norm.reshape(1, D_MODEL)

    inv_freq = ROPE_THETA ** (-jnp.arange(0, ROPE_DIM, 2, dtype=_F32) / ROPE_DIM)
    invf = jnp.tile(inv_freq, LANES // (ROPE_DIM // 2)).reshape(1, LANES)
    pos = positions.astype(_F32)[..., None]
    cos, sin_lo, sin_hi = _rope_tables(pos, invf)

    xs = x.reshape(n, D_MODEL)
    for layer in range(DEPTH):
        xs = _ffn(xs, g1, wg1, wu1, wd1, gf, layer, False)
        v_pool, nat, mid, far = _inproj(xs.reshape(b, s, D_MODEL), gm, w_in_b, cos, sin_lo,
                                        sin_hi, layer)
        y_pool = _pool(v_pool, pool_w_bd, ps, layer)
        y_attn = _attention(nat, mid, far)
        xs = _outproj(xs, y_pool.reshape(n, POOL_WIDTH), y_attn.reshape(n, ATTN_WIDTH),
                      w_out_b, layer)
        xs = _ffn(xs, g2, wg2, wu2, wd2, gf, layer, layer == DEPTH - 1)
    return xs.reshape(b, s, D_MODEL)


def kernel(x, positions, ffn1_norm, ffn1_w_gate, ffn1_w_up, ffn1_w_down, mix_norm, w_in,
           pool_w, pool_scale, w_out, ffn2_norm, ffn2_w_gate, ffn2_w_up, ffn2_w_down,
           final_norm):
    return _forward(x, positions, ffn1_norm, ffn1_w_gate, ffn1_w_up, ffn1_w_down, mix_norm,
                    w_in, pool_w, pool_scale, w_out, ffn2_norm, ffn2_w_gate, ffn2_w_up,
                    ffn2_w_down, final_norm)
```

```python
import functools

import jax
import jax.numpy as jnp
from jax import lax
from jax.experimental import pallas as pl
from jax.experimental.pallas import tpu as pltpu

D_MODEL = 1024
DEPTH = 4
POOL_WIDTH = 256
POOL_WINDOWS = (2, 4, 8, 16)
POOL_GROUP_DIM = 64
HEAD_DIM = 64
ATTN_WIDTH = 768
DILATED_PATTERNS = ((128, 1), (512, 4), (2048, 16))
ROPE_THETA = 500000.0
ROPE_DIM = 16
D_FF = 2816
IN_PROJ_WIDTH = 2560
NORM_EPS = 1e-6
MASK_VALUE = -1e30
LOG2_E = 1.4426950408889634

LANES = 128
MXU_WIDTH = 256
ROW_TILE = 512
FF_CHUNK = MXU_WIDTH
N_FF_CHUNKS = D_FF // FF_CHUNK
Q_BLOCK = 128
BAND_HALF = 64
ATTN_SLOTS = 3
MID_DIL = 4
QKV_WIDTH = 3 * ATTN_WIDTH
POOL_PAD = 16
CAST_ROWS = 256
VMEM_LIMIT = 56 * 1024 * 1024

_F32 = jnp.float32
_BF16 = jnp.bfloat16

assert all(w // (2 * d) == BAND_HALF for w, d in DILATED_PATTERNS)
assert [d for _, d in DILATED_PATTERNS] == [1, MID_DIL, MID_DIL * MID_DIL]


def _params(n_grid_dims):
    return pltpu.CompilerParams(
        dimension_semantics=("arbitrary",) * n_grid_dims,
        vmem_limit_bytes=VMEM_LIMIT)


def _resident(block_shape, index_map):
    return pl.BlockSpec(block_shape, index_map, pipeline_mode=pl.Buffered(1))


def _rmsnorm(x, g):
    y = x * lax.rsqrt(jnp.mean(x * x, axis=-1, keepdims=True) + NORM_EPS)
    return y * g


def _cast_kernel(w_ref, o_ref):
    o_ref[...] = w_ref[...].astype(o_ref.dtype)


def _to_bf16(w):
    depth, rows, cols = w.shape
    spec = pl.BlockSpec((None, CAST_ROWS, cols), lambda l, i: (l, i, 0))
    return pl.pallas_call(
        _cast_kernel,
        grid=(depth, rows // CAST_ROWS),
        in_specs=[spec],
        out_specs=spec,
        out_shape=jax.ShapeDtypeStruct(w.shape, _BF16),
        compiler_params=_params(2),
        name="to_bf16",
    )(w)


def _ffn_kernel(*refs, mix_in, apply_final_norm):
    if mix_in:
        x_ref, yp_ref, ya_ref, wo_ref = refs[:4]
        refs = refs[4:]
    else:
        x_ref = refs[0]
        refs = refs[1:]
    g_ref, wg_ref, wu_ref, wd_ref, fg_ref, o_ref, h_ref, acc_ref = refs
    x = x_ref[...]
    if mix_in:
        x = x + jnp.dot(yp_ref[...], wo_ref[0:POOL_WIDTH, :], preferred_element_type=_F32)
        x = x + jnp.dot(ya_ref[...], wo_ref[POOL_WIDTH:D_MODEL, :], preferred_element_type=_F32)
        o_ref[...] = x
    h_ref[...] = _rmsnorm(x, g_ref[...]).astype(_BF16)

    for c in range(N_FF_CHUNKS):
        cols = slice(c * FF_CHUNK, (c + 1) * FF_CHUNK)
        h = h_ref[...]
        gate = jnp.dot(h, wg_ref[:, cols], preferred_element_type=_F32)
        up = jnp.dot(h, wu_ref[:, cols], preferred_element_type=_F32)
        act = ((gate * jax.nn.sigmoid(gate)) * up).astype(_BF16)
        down = jnp.dot(act, wd_ref[cols, :], preferred_element_type=_F32)
        if c == 0:
            acc_ref[...] = down
        elif c < N_FF_CHUNKS - 1:
            acc_ref[...] += down
    base = o_ref[...] if mix_in else x_ref[...]
    y = base + 0.5 * (acc_ref[...] + down)
    if apply_final_norm:
        y = _rmsnorm(y, fg_ref[...])
    o_ref[...] = y


def _ffn(x, gains, wg, wu, wd, final_gain, layer, apply_final_norm, mixer=None):
    n = x.shape[0]
    rows = lambda width: pl.BlockSpec((ROW_TILE, width), lambda i: (i, 0))
    operands = [x]
    in_specs = [rows(D_MODEL)]
    if mixer is not None:
        operands += list(mixer)
        in_specs += [rows(POOL_WIDTH), rows(ATTN_WIDTH),
                     _resident((None, D_MODEL, D_MODEL), lambda i: (layer, 0, 0))]
    operands += [gains, wg, wu, wd, final_gain]
    in_specs += [
        pl.BlockSpec((None, 1, D_MODEL), lambda i: (layer, 0, 0)),
        _resident((None, D_MODEL, D_FF), lambda i: (layer, 0, 0)),
        _resident((None, D_MODEL, D_FF), lambda i: (layer, 0, 0)),
        _resident((None, D_FF, D_MODEL), lambda i: (layer, 0, 0)),
        pl.BlockSpec((1, D_MODEL), lambda i: (0, 0)),
    ]
    return pl.pallas_call(
        functools.partial(_ffn_kernel, mix_in=mixer is not None,
                          apply_final_norm=apply_final_norm),
        grid=(n // ROW_TILE,),
        in_specs=in_specs,
        out_specs=rows(D_MODEL),
        out_shape=jax.ShapeDtypeStruct((n, D_MODEL), _F32),
        scratch_shapes=[pltpu.VMEM((ROW_TILE, D_MODEL), _BF16),
                        pltpu.VMEM((ROW_TILE, D_MODEL), _F32)],
        compiler_params=_params(1),
        name="ffn",
    )(*operands)


def _rope_table_kernel(pos_ref, invf_ref, cos_ref, sin_lo_ref, sin_hi_ref):
    ang = pos_ref[...] * invf_ref[...]
    c = jnp.cos(ang)
    s = jnp.sin(ang)
    d = lax.broadcasted_iota(jnp.int32, ang.shape, 1) & (HEAD_DIM - 1)
    half = ROPE_DIM // 2
    cos_ref[...] = jnp.where(d < ROPE_DIM, c, 1.0)
    sin_lo_ref[...] = jnp.where(d < half, -s, 0.0)
    sin_hi_ref[...] = jnp.where((d >= half) & (d < ROPE_DIM), s, 0.0)


def _rope_tables(pos, invf):
    b, s, _ = pos.shape
    out = jax.ShapeDtypeStruct((b, s, LANES), _F32)
    spec = pl.BlockSpec((None, s, LANES), lambda i: (i, 0, 0))
    return pl.pallas_call(
        _rope_table_kernel,
        grid=(b,),
        in_specs=[pl.BlockSpec((None, s, 1), lambda i: (i, 0, 0)),
                  pl.BlockSpec((1, LANES), lambda i: (0, 0))],
        out_specs=[spec, spec, spec],
        out_shape=[out, out, out],
        compiler_params=_params(1),
        name="rope_tables",
    )(pos, invf)


def _inproj_kernel(x_ref, g_ref, w_ref, cos_ref, sin_lo_ref, sin_hi_ref,
                   vpool_ref, nat_ref, mid_ref, far_ref, slab_ref, midslab_ref):
    h = _rmsnorm(x_ref[...], g_ref[...]).astype(_BF16)
    cos = cos_ref[...]
    sin_lo = sin_lo_ref[...]
    sin_hi = sin_hi_ref[...]
    half_rope = ROPE_DIM // 2
    n_heads_tiles = ATTN_WIDTH // LANES

    def rope(t):
        return (t * cos + pltpu.roll(t, LANES - half_rope, 1) * sin_lo
                + pltpu.roll(t, half_rope, 1) * sin_hi)

    for c in range(IN_PROJ_WIDTH // MXU_WIDTH):
        cols = slice(c * MXU_WIDTH, (c + 1) * MXU_WIDTH)
        part = jnp.dot(h, w_ref[:, cols], preferred_element_type=_F32)
        if c * MXU_WIDTH < POOL_WIDTH:
            vpool_ref[:, cols] = part
            continue
        for half in range(MXU_WIDTH // LANES):
            j = (c * MXU_WIDTH - POOL_WIDTH) // LANES + half
            t = part[:, half * LANES:(half + 1) * LANES]
            if j < n_heads_tiles:
                t = rope(t) * (LOG2_E / (HEAD_DIM ** 0.5))
            elif j < 2 * n_heads_tiles:
                t = rope(t)
            slab_ref[j] = t
            nat_ref[:, j * LANES:(j + 1) * LANES] = t.astype(_BF16)

    far_dil = MID_DIL * MID_DIL
    for j in range(QKV_WIDTH // LANES):
        lanes = slice(j * LANES, (j + 1) * LANES)
        for r in range(MID_DIL):
            rows = pl.ds(r, ROW_TILE // MID_DIL, stride=MID_DIL)
            picked = slab_ref[j, rows, :]
            mid_ref[r, :, lanes] = picked.astype(_BF16)
            midslab_ref[j, r] = picked
        for r in range(far_dil):
            rows = pl.ds(r // MID_DIL, ROW_TILE // far_dil, stride=MID_DIL)
            far_ref[r, :, lanes] = midslab_ref[j, r % MID_DIL, rows, :].astype(_BF16)


def _inproj(x, gains, w_in, cos, sin_lo, sin_hi, layer):
    b, s, _ = x.shape
    tiles = s // ROW_TILE
    far_dil = MID_DIL * MID_DIL
    table = pl.BlockSpec((None, ROW_TILE, LANES), lambda i, t: (i, t, 0))
    return pl.pallas_call(
        _inproj_kernel,
        grid=(b, tiles),
        in_specs=[
            pl.BlockSpec((None, ROW_TILE, D_MODEL), lambda i, t: (i, t, 0)),
            pl.BlockSpec((None, 1, D_MODEL), lambda i, t: (layer, 0, 0)),
            _resident((None, D_MODEL, IN_PROJ_WIDTH), lambda i, t: (layer, 0, 0)),
            table, table, table,
        ],
        out_specs=[
            pl.BlockSpec((None, ROW_TILE, POOL_WIDTH), lambda i, t: (i, t, 0)),
            pl.BlockSpec((None, ROW_TILE, QKV_WIDTH), lambda i, t: (i, t, 0)),
            pl.BlockSpec((None, MID_DIL, ROW_TILE // MID_DIL, QKV_WIDTH),
                         lambda i, t: (i, 0, t, 0)),
            pl.BlockSpec((None, far_dil, ROW_TILE // far_dil, QKV_WIDTH),
                         lambda i, t: (i, 0, t, 0)),
        ],
        out_shape=[
            jax.ShapeDtypeStruct((b, s, POOL_WIDTH), _F32),
            jax.ShapeDtypeStruct((b, s, QKV_WIDTH), _BF16),
            jax.ShapeDtypeStruct((b, MID_DIL, s // MID_DIL, QKV_WIDTH), _BF16),
            jax.ShapeDtypeStruct((b, far_dil, s // far_dil, QKV_WIDTH), _BF16),
        ],
        scratch_shapes=[
            pltpu.VMEM((QKV_WIDTH // LANES, ROW_TILE, LANES), _F32),
            pltpu.VMEM((QKV_WIDTH // LANES, MID_DIL, ROW_TILE // MID_DIL, LANES), _F32)],
        compiler_params=_params(2),
        name="inproj",
    )(x, gains, w_in, cos, sin_lo, sin_hi)


def _pool_kernel(v_ref, w_ref, scale_ref, o_ref, pad_ref):
    s = v_ref.shape[0]
    tile = pl.program_id(1)
    v = v_ref[...]
    pad_ref[...] = jnp.zeros_like(pad_ref)
    pad_ref[POOL_PAD:POOL_PAD + s, :] = v
    x = pad_ref[...]
    rows = x.shape[0]

    def shifted(t, k):
        return pltpu.roll(t, k % rows, 0)

    w2 = x + shifted(x, 1)
    w4 = shifted(w2, 1) + shifted(w2, -1)
    w8 = shifted(w4, 2) + shifted(w4, -2)
    w16 = shifted(w8, 4) + shifted(w8, -4)
    first_group = lax.broadcasted_iota(jnp.int32, x.shape, 1) < POOL_GROUP_DIM
    small = jnp.where(first_group, w2, w4)
    large = jnp.where(first_group, w8, w16)
    wsum = jnp.where(tile == 0, small, large)[POOL_PAD:POOL_PAD + s, :]

    lane_first = lax.broadcasted_iota(jnp.int32, v.shape, 1) < POOL_GROUP_DIM
    half_w = jnp.where(tile == 0, jnp.where(lane_first, 1, 2), jnp.where(lane_first, 4, 8))
    pos = lax.broadcasted_iota(jnp.int32, v.shape, 0)
    lo = jnp.maximum(pos - half_w, 0)
    hi = jnp.minimum(pos + half_w - 1, s - 1)
    cnt = (hi - lo + 1).astype(_F32)
    diff = wsum / cnt - v
    y = jnp.dot(diff.astype(_BF16), w_ref[...], preferred_element_type=_F32)
    o_ref[...] = (y * scale_ref[...]).astype(o_ref.dtype)


def _pool(v_pool, pool_w_bd, pool_scale, layer):
    b, s, _ = v_pool.shape
    n_tiles = POOL_WIDTH // LANES
    return pl.pallas_call(
        _pool_kernel,
        grid=(b, n_tiles),
        in_specs=[
            pl.BlockSpec((None, s, LANES), lambda i, j: (i, 0, j)),
            pl.BlockSpec((None, None, LANES, LANES), lambda i, j: (layer, j, 0, 0)),
            pl.BlockSpec((None, 1, LANES), lambda i, j: (layer, 0, j)),
        ],
        out_specs=pl.BlockSpec((None, s, LANES), lambda i, j: (i, 0, j)),
        out_shape=jax.ShapeDtypeStruct((b, s, POOL_WIDTH), _BF16),
        scratch_shapes=[pltpu.VMEM((s + 2 * POOL_PAD, LANES), _F32)],
        compiler_params=_params(2),
        name="pool",
    )(v_pool, pool_w_bd, pool_scale)


def _attn_kernel(qn_ref, kn_ref, vn_ref, qm_ref, km_ref, vm_ref, qf_ref, kf_ref, vf_ref, o_ref,
                 mask_ref, s_ref, p_ref, st_ref, num_far, m_far, l_far, num_mid, m_mid, l_mid):
    s_len = qn_ref.shape[1]
    far_dil = MID_DIL * MID_DIL
    operands = ((qn_ref, kn_ref, vn_ref), (qm_ref, km_ref, vm_ref), (qf_ref, kf_ref, vf_ref))
    first = lax.broadcasted_iota(jnp.int32, (Q_BLOCK, LANES), 1) < HEAD_DIM

    rr = lax.broadcasted_iota(jnp.int32, (Q_BLOCK, 2 * Q_BLOCK), 0)
    cc = lax.broadcasted_iota(jnp.int32, (Q_BLOCK, 2 * Q_BLOCK), 1)
    for idx in range(3):
        off = (idx - 2) * BAND_HALF
        mask_ref[idx] = (jnp.abs(cc - rr + off) <= BAND_HALF).astype(_F32)

    blocks = []
    for order, n_seq in ((2, far_dil), (1, MID_DIL), (0, 1)):
        seq = s_len // n_seq
        blocks += [(order, r, q0, seq) for r in range(n_seq) for q0 in range(0, seq, Q_BLOCK)]

    def key_window(q0, seq):
        kwin = min(2 * Q_BLOCK, seq)
        return min(max(q0 - BAND_HALF, 0), seq - kwin), kwin

    def scores(blk, slot):
        order, r, q0, seq = blk
        q_ref, k_ref, _ = operands[order]
        k0, kwin = key_window(q0, seq)
        q = q_ref[r, q0:q0 + Q_BLOCK, :]
        kblk = k_ref[r, k0:k0 + kwin, :]
        valid = mask_ref[(k0 - q0) // BAND_HALF + 2, :, 0:kwin] != 0.0
        zero = jnp.zeros_like(q)
        for h, qh in enumerate((jnp.where(first, q, zero), jnp.where(first, zero, q))):
            sc = lax.dot_general(qh, kblk, (((1,), (1,)), ((), ())),
                                 preferred_element_type=_F32)
            s_ref[slot, h, :, 0:kwin] = jnp.where(valid, sc, MASK_VALUE)

    def softmax(blk, slot):
        _, kwin = key_window(blk[2], blk[3])
        stats = []
        for h in range(2):
            sc = s_ref[slot, h, :, 0:kwin]
            m = jnp.max(sc, axis=-1, keepdims=True)
            p = jnp.exp2(sc - m)
            stats.append((m, jnp.sum(p, axis=-1, keepdims=True)))
            p_ref[slot, h, :, 0:kwin] = p.astype(_BF16)
        st_ref[slot, 0] = jnp.where(first, stats[0][0], stats[1][0])
        st_ref[slot, 1] = jnp.where(first, stats[0][1], stats[1][1])

    def values(blk, slot):
        order, r, q0, seq = blk
        v_ref = operands[order][2]
        k0, kwin = key_window(q0, seq)
        vblk = v_ref[r, k0:k0 + kwin, :]
        o = [jnp.dot(p_ref[slot, h, :, 0:kwin], vblk, preferred_element_type=_F32)
             for h in range(2)]
        num = jnp.where(first, o[0], o[1])
        m = st_ref[slot, 0]
        l = st_ref[slot, 1]
        rows = slice(q0, q0 + Q_BLOCK)
        if order == 2:
            out_rows = pl.ds(MID_DIL * q0 + r // MID_DIL, Q_BLOCK, stride=MID_DIL)
            num_far[r % MID_DIL, out_rows, :] = num
            m_far[r % MID_DIL, out_rows, :] = m
            l_far[r % MID_DIL, out_rows, :] = l
            return
        if order == 1:
            m_c, num_c, l_c = m_far[r, rows, :], num_far[r, rows, :], l_far[r, rows, :]
        else:
            m_c, num_c, l_c = m_mid[rows, :], num_mid[rows, :], l_mid[rows, :]
        m_new = jnp.maximum(m, m_c)
        a = jnp.exp2(m - m_new)
        a_c = jnp.exp2(m_c - m_new)
        num = a * num + a_c * num_c
        l = a * l + a_c * l_c
        if order == 1:
            out_rows = pl.ds(MID_DIL * q0 + r, Q_BLOCK, stride=MID_DIL)
            num_mid[out_rows, :] = num
            m_mid[out_rows, :] = m_new
            l_mid[out_rows, :] = l
        else:
            o_ref[rows, :] = (num / l).astype(o_ref.dtype)

    stages = (scores, softmax, values)
    for step in range(len(blocks) + len(stages) - 1):
        for depth, stage in enumerate(stages):
            i = step - depth
            if 0 <= i < len(blocks):
                stage(blocks[i], i % ATTN_SLOTS)


def _attention(nat, mid, far):
    b, s, _ = nat.shape
    n_pairs = ATTN_WIDTH // LANES
    far_dil = MID_DIL * MID_DIL

    def specs(n_seq):
        def tile(off):
            return pl.BlockSpec((None, n_seq, s // n_seq, LANES),
                                lambda i, j: (i, 0, 0, off + j))
        return [tile(0), tile(n_pairs), tile(2 * n_pairs)]

    return pl.pallas_call(
        _attn_kernel,
        grid=(b, n_pairs),
        in_specs=specs(1) + specs(MID_DIL) + specs(far_dil),
        out_specs=pl.BlockSpec((None, s, LANES), lambda i, j: (i, 0, j)),
        out_shape=jax.ShapeDtypeStruct((b, s, ATTN_WIDTH), _BF16),
        scratch_shapes=[pltpu.VMEM((3, Q_BLOCK, 2 * Q_BLOCK), _F32),
                        pltpu.VMEM((ATTN_SLOTS, 2, Q_BLOCK, 2 * Q_BLOCK), _F32),
                        pltpu.VMEM((ATTN_SLOTS, 2, Q_BLOCK, 2 * Q_BLOCK), _BF16),
                        pltpu.VMEM((ATTN_SLOTS, 2, Q_BLOCK, LANES), _F32)]
        + [pltpu.VMEM((MID_DIL, s // MID_DIL, LANES), _F32)] * 3
        + [pltpu.VMEM((s, LANES), _F32)] * 3,
        compiler_params=_params(2),
        name="dilated_attention",
    )(nat.reshape(b, 1, s, QKV_WIDTH), nat.reshape(b, 1, s, QKV_WIDTH),
      nat.reshape(b, 1, s, QKV_WIDTH), mid, mid, mid, far, far, far)


def _pool_block_diag(pool_w):
    g = POOL_GROUP_DIM
    tiles = POOL_WIDTH // LANES
    per_tile = LANES // g
    z = jnp.zeros((DEPTH, tiles, LANES, LANES), pool_w.dtype)
    for t in range(tiles):
        for h in range(per_tile):
            z = z.at[:, t, h * g:(h + 1) * g, h * g:(h + 1) * g].set(pool_w[:, per_tile * t + h])
    return z.astype(_BF16)


@jax.jit
def _forward(x, positions, ffn1_norm, ffn1_w_gate, ffn1_w_up, ffn1_w_down, mix_norm, w_in,
             pool_w, pool_scale, w_out, ffn2_norm, ffn2_w_gate, ffn2_w_up, ffn2_w_down,
             final_norm):
    b, s, d = x.shape
    n = b * s
    assert d == D_MODEL and s % ROW_TILE == 0 and s % (Q_BLOCK * MID_DIL * MID_DIL) == 0
    assert all(w // 2 == POOL_WINDOWS[0] * 2 ** i // 2 for i, w in enumerate(POOL_WINDOWS))

    wg1, wu1, wd1 = (_to_bf16(w) for w in (ffn1_w_gate, ffn1_w_up, ffn1_w_down))
    wg2, wu2, wd2 = (_to_bf16(w) for w in (ffn2_w_gate, ffn2_w_up, ffn2_w_down))
    w_in_b = _to_bf16(w_in)
    w_out_b = _to_bf16(w_out)
    pool_w_bd = _pool_block_diag(pool_w)
    g1 = ffn1_norm.reshape(DEPTH, 1, D_MODEL)
    g2 = ffn2_norm.reshape(DEPTH, 1, D_MODEL)
    gm = mix_norm.reshape(DEPTH, 1, D_MODEL)
    ps = pool_scale.reshape(DEPTH, 1, POOL_WIDTH)
    gf = final_norm.reshape(1, D_MODEL)

    inv_freq = ROPE_THETA ** (-jnp.arange(0, ROPE_DIM, 2, dtype=_F32) / ROPE_DIM)
    invf = jnp.tile(inv_freq, LANES // (ROPE_DIM // 2)).reshape(1, LANES)
    pos = positions.astype(_F32)[..., None]
    cos, sin_lo, sin_hi = _rope_tables(pos, invf)

    xs = x.reshape(n, D_MODEL)
    for layer in range(DEPTH):
        xs = _ffn(xs, g1, wg1, wu1, wd1, gf, layer, False)
        v_pool, nat, mid, far = _inproj(xs.reshape(b, s, D_MODEL), gm, w_in_b, cos, sin_lo,
                                        sin_hi, layer)
        y_pool = _pool(v_pool, pool_w_bd, ps, layer)
        y_attn = _attention(nat, mid, far)
        mixer = (y_pool.reshape(n, POOL_WIDTH), y_attn.reshape(n, ATTN_WIDTH), w_out_b)
        xs = _ffn(xs, g2, wg2, wu2, wd2, gf, layer, layer == DEPTH - 1, mixer)
    return xs.reshape(b, s, D_MODEL)


def kernel(x, positions, ffn1_norm, ffn1_w_gate, ffn1_w_up, ffn1_w_down, mix_norm, w_in,
           pool_w, pool_scale, w_out, ffn2_norm, ffn2_w_gate, ffn2_w_up, ffn2_w_down,
           final_norm):
    return _forward(x, positions, ffn1_norm, ffn1_w_gate, ffn1_w_up, ffn1_w_down, mix_norm,
                    w_in, pool_w, pool_scale, w_out, ffn2_norm, ffn2_w_gate, ffn2_w_up,
                    ffn2_w_down, final_norm)
```

```python
import functools

import jax
import jax.numpy as jnp
from jax import lax
from jax.experimental import pallas as pl
from jax.experimental.pallas import tpu as pltpu

D_MODEL = 1024
DEPTH = 4
POOL_WIDTH = 256
POOL_WINDOWS = (2, 4, 8, 16)
POOL_GROUP_DIM = 64
HEAD_DIM = 64
ATTN_WIDTH = 768
DILATED_PATTERNS = ((128, 1), (512, 4), (2048, 16))
ROPE_THETA = 500000.0
ROPE_DIM = 16
D_FF = 2816
IN_PROJ_WIDTH = 2560
NORM_EPS = 1e-6
MASK_VALUE = -1e30
LOG2_E = 1.4426950408889634

LANES = 128
MXU_WIDTH = 256
ROW_TILE = 512
FF_CHUNK = MXU_WIDTH
N_FF_CHUNKS = D_FF // FF_CHUNK
Q_BLOCK = 128
BAND_HALF = 64
ATTN_SLOTS = 3
MID_DIL = 4
QKV_WIDTH = 3 * ATTN_WIDTH
POOL_PAD = 16
VMEM_LIMIT = 60 * 1024 * 1024

_F32 = jnp.float32
_BF16 = jnp.bfloat16

assert all(w // (2 * d) == BAND_HALF for w, d in DILATED_PATTERNS)
assert [d for _, d in DILATED_PATTERNS] == [1, MID_DIL, MID_DIL * MID_DIL]


def _params(n_grid_dims):
    return pltpu.CompilerParams(
        dimension_semantics=("arbitrary",) * n_grid_dims,
        vmem_limit_bytes=VMEM_LIMIT)


def _resident(block_shape, index_map):
    return pl.BlockSpec(block_shape, index_map, pipeline_mode=pl.Buffered(1))


def _rmsnorm(x, g):
    y = x * lax.rsqrt(jnp.mean(x * x, axis=-1, keepdims=True) + NORM_EPS)
    return y * g


def _ffn_kernel(*refs, mix_in, apply_final_norm):
    if mix_in:
        x_ref, yp_ref, ya_ref, wo_ref = refs[:4]
        refs = refs[4:]
    else:
        x_ref = refs[0]
        refs = refs[1:]
    g_ref, wg_ref, wu_ref, wd_ref, fg_ref, o_ref, h_ref, acc_ref = refs
    x = x_ref[...]
    if mix_in:
        x = x + jnp.dot(yp_ref[...], wo_ref[0:POOL_WIDTH, :].astype(_BF16),
                        preferred_element_type=_F32)
        x = x + jnp.dot(ya_ref[...], wo_ref[POOL_WIDTH:D_MODEL, :].astype(_BF16),
                        preferred_element_type=_F32)
        o_ref[...] = x
    h_ref[...] = _rmsnorm(x, g_ref[...]).astype(_BF16)

    for c in range(N_FF_CHUNKS):
        cols = slice(c * FF_CHUNK, (c + 1) * FF_CHUNK)
        h = h_ref[...]
        gate = jnp.dot(h, wg_ref[:, cols].astype(_BF16), preferred_element_type=_F32)
        up = jnp.dot(h, wu_ref[:, cols].astype(_BF16), preferred_element_type=_F32)
        act = ((gate * jax.nn.sigmoid(gate)) * up).astype(_BF16)
        down = jnp.dot(act, wd_ref[cols, :].astype(_BF16), preferred_element_type=_F32)
        if c == 0:
            acc_ref[...] = down
        elif c < N_FF_CHUNKS - 1:
            acc_ref[...] += down
    base = o_ref[...] if mix_in else x_ref[...]
    y = base + 0.5 * (acc_ref[...] + down)
    if apply_final_norm:
        y = _rmsnorm(y, fg_ref[...])
    o_ref[...] = y


def _ffn(x, gains, wg, wu, wd, final_gain, layer, apply_final_norm, mixer=None):
    n = x.shape[0]
    rows = lambda width: pl.BlockSpec((ROW_TILE, width), lambda i: (i, 0))
    operands = [x]
    in_specs = [rows(D_MODEL)]
    if mixer is not None:
        operands += list(mixer)
        in_specs += [rows(POOL_WIDTH), rows(ATTN_WIDTH),
                     _resident((None, D_MODEL, D_MODEL), lambda i: (layer, 0, 0))]
    operands += [gains, wg, wu, wd, final_gain]
    in_specs += [
        pl.BlockSpec((None, 1, D_MODEL), lambda i: (layer, 0, 0)),
        _resident((None, D_MODEL, D_FF), lambda i: (layer, 0, 0)),
        _resident((None, D_MODEL, D_FF), lambda i: (layer, 0, 0)),
        _resident((None, D_FF, D_MODEL), lambda i: (layer, 0, 0)),
        pl.BlockSpec((1, D_MODEL), lambda i: (0, 0)),
    ]
    return pl.pallas_call(
        functools.partial(_ffn_kernel, mix_in=mixer is not None,
                          apply_final_norm=apply_final_norm),
        grid=(n // ROW_TILE,),
        in_specs=in_specs,
        out_specs=rows(D_MODEL),
        out_shape=jax.ShapeDtypeStruct((n, D_MODEL), _F32),
        scratch_shapes=[pltpu.VMEM((ROW_TILE, D_MODEL), _BF16),
                        pltpu.VMEM((ROW_TILE, D_MODEL), _F32)],
        compiler_params=_params(1),
        name="ffn",
    )(*operands)


def _rope_table_kernel(pos_ref, invf_ref, cos_ref, sin_lo_ref, sin_hi_ref):
    ang = pos_ref[...] * invf_ref[...]
    c = jnp.cos(ang)
    s = jnp.sin(ang)
    d = lax.broadcasted_iota(jnp.int32, ang.shape, 1) & (HEAD_DIM - 1)
    half = ROPE_DIM // 2
    cos_ref[...] = jnp.where(d < ROPE_DIM, c, 1.0)
    sin_lo_ref[...] = jnp.where(d < half, -s, 0.0)
    sin_hi_ref[...] = jnp.where((d >= half) & (d < ROPE_DIM), s, 0.0)


def _rope_tables(pos, invf):
    b, s, _ = pos.shape
    out = jax.ShapeDtypeStruct((b, s, LANES), _F32)
    spec = pl.BlockSpec((None, s, LANES), lambda i: (i, 0, 0))
    return pl.pallas_call(
        _rope_table_kernel,
        grid=(b,),
        in_specs=[pl.BlockSpec((None, s, 1), lambda i: (i, 0, 0)),
                  pl.BlockSpec((1, LANES), lambda i: (0, 0))],
        out_specs=[spec, spec, spec],
        out_shape=[out, out, out],
        compiler_params=_params(1),
        name="rope_tables",
    )(pos, invf)


def _inproj_kernel(x_ref, g_ref, w_ref, cos_ref, sin_lo_ref, sin_hi_ref,
                   vpool_ref, nat_ref, mid_ref, far_ref, slab_ref, midslab_ref):
    h = _rmsnorm(x_ref[...], g_ref[...]).astype(_BF16)
    cos = cos_ref[...]
    sin_lo = sin_lo_ref[...]
    sin_hi = sin_hi_ref[...]
    half_rope = ROPE_DIM // 2
    n_heads_tiles = ATTN_WIDTH // LANES

    def rope(t):
        return (t * cos + pltpu.roll(t, LANES - half_rope, 1) * sin_lo
                + pltpu.roll(t, half_rope, 1) * sin_hi)

    for c in range(IN_PROJ_WIDTH // MXU_WIDTH):
        cols = slice(c * MXU_WIDTH, (c + 1) * MXU_WIDTH)
        part = jnp.dot(h, w_ref[:, cols].astype(_BF16), preferred_element_type=_F32)
        if c * MXU_WIDTH < POOL_WIDTH:
            vpool_ref[:, cols] = part
            continue
        for half in range(MXU_WIDTH // LANES):
            j = (c * MXU_WIDTH - POOL_WIDTH) // LANES + half
            t = part[:, half * LANES:(half + 1) * LANES]
            if j < n_heads_tiles:
                t = rope(t) * (LOG2_E / (HEAD_DIM ** 0.5))
            elif j < 2 * n_heads_tiles:
                t = rope(t)
            slab_ref[j] = t
            nat_ref[:, j * LANES:(j + 1) * LANES] = t.astype(_BF16)

    far_dil = MID_DIL * MID_DIL
    for j in range(QKV_WIDTH // LANES):
        lanes = slice(j * LANES, (j + 1) * LANES)
        for r in range(MID_DIL):
            rows = pl.ds(r, ROW_TILE // MID_DIL, stride=MID_DIL)
            picked = slab_ref[j, rows, :]
            mid_ref[r, :, lanes] = picked.astype(_BF16)
            midslab_ref[j, r] = picked
        for r in range(far_dil):
            rows = pl.ds(r // MID_DIL, ROW_TILE // far_dil, stride=MID_DIL)
            far_ref[r, :, lanes] = midslab_ref[j, r % MID_DIL, rows, :].astype(_BF16)


def _inproj(x, gains, w_in, cos, sin_lo, sin_hi, layer):
    b, s, _ = x.shape
    tiles = s // ROW_TILE
    far_dil = MID_DIL * MID_DIL
    table = pl.BlockSpec((None, ROW_TILE, LANES), lambda i, t: (i, t, 0))
    return pl.pallas_call(
        _inproj_kernel,
        grid=(b, tiles),
        in_specs=[
            pl.BlockSpec((None, ROW_TILE, D_MODEL), lambda i, t: (i, t, 0)),
            pl.BlockSpec((None, 1, D_MODEL), lambda i, t: (layer, 0, 0)),
            _resident((None, D_MODEL, IN_PROJ_WIDTH), lambda i, t: (layer, 0, 0)),
            table, table, table,
        ],
        out_specs=[
            pl.BlockSpec((None, ROW_TILE, POOL_WIDTH), lambda i, t: (i, t, 0)),
            pl.BlockSpec((None, ROW_TILE, QKV_WIDTH), lambda i, t: (i, t, 0)),
            pl.BlockSpec((None, MID_DIL, ROW_TILE // MID_DIL, QKV_WIDTH),
                         lambda i, t: (i, 0, t, 0)),
            pl.BlockSpec((None, far_dil, ROW_TILE // far_dil, QKV_WIDTH),
                         lambda i, t: (i, 0, t, 0)),
        ],
        out_shape=[
            jax.ShapeDtypeStruct((b, s, POOL_WIDTH), _F32),
            jax.ShapeDtypeStruct((b, s, QKV_WIDTH), _BF16),
            jax.ShapeDtypeStruct((b, MID_DIL, s // MID_DIL, QKV_WIDTH), _BF16),
            jax.ShapeDtypeStruct((b, far_dil, s // far_dil, QKV_WIDTH), _BF16),
        ],
        scratch_shapes=[
            pltpu.VMEM((QKV_WIDTH // LANES, ROW_TILE, LANES), _F32),
            pltpu.VMEM((QKV_WIDTH // LANES, MID_DIL, ROW_TILE // MID_DIL, LANES), _F32)],
        compiler_params=_params(2),
        name="inproj",
    )(x, gains, w_in, cos, sin_lo, sin_hi)


def _pool_kernel(v_ref, w_ref, scale_ref, o_ref, pad_ref):
    s = v_ref.shape[0]
    tile = pl.program_id(1)
    v = v_ref[...]
    pad_ref[...] = jnp.zeros_like(pad_ref)
    pad_ref[POOL_PAD:POOL_PAD + s, :] = v
    x = pad_ref[...]
    rows = x.shape[0]

    def shifted(t, k):
        return pltpu.roll(t, k % rows, 0)

    w2 = x + shifted(x, 1)
    w4 = shifted(w2, 1) + shifted(w2, -1)
    w8 = shifted(w4, 2) + shifted(w4, -2)
    w16 = shifted(w8, 4) + shifted(w8, -4)
    first_group = lax.broadcasted_iota(jnp.int32, x.shape, 1) < POOL_GROUP_DIM
    small = jnp.where(first_group, w2, w4)
    large = jnp.where(first_group, w8, w16)
    wsum = jnp.where(tile == 0, small, large)[POOL_PAD:POOL_PAD + s, :]

    lane_first = lax.broadcasted_iota(jnp.int32, v.shape, 1) < POOL_GROUP_DIM
    half_w = jnp.where(tile == 0, jnp.where(lane_first, 1, 2), jnp.where(lane_first, 4, 8))
    pos = lax.broadcasted_iota(jnp.int32, v.shape, 0)
    lo = jnp.maximum(pos - half_w, 0)
    hi = jnp.minimum(pos + half_w - 1, s - 1)
    cnt = (hi - lo + 1).astype(_F32)
    diff = wsum / cnt - v
    y = jnp.dot(diff.astype(_BF16), w_ref[...], preferred_element_type=_F32)
    o_ref[...] = (y * scale_ref[...]).astype(o_ref.dtype)


def _pool(v_pool, pool_w_bd, pool_scale, layer):
    b, s, _ = v_pool.shape
    n_tiles = POOL_WIDTH // LANES
    return pl.pallas_call(
        _pool_kernel,
        grid=(b, n_tiles),
        in_specs=[
            pl.BlockSpec((None, s, LANES), lambda i, j: (i, 0, j)),
            pl.BlockSpec((None, None, LANES, LANES), lambda i, j: (layer, j, 0, 0)),
            pl.BlockSpec((None, 1, LANES), lambda i, j: (layer, 0, j)),
        ],
        out_specs=pl.BlockSpec((None, s, LANES), lambda i, j: (i, 0, j)),
        out_shape=jax.ShapeDtypeStruct((b, s, POOL_WIDTH), _BF16),
        scratch_shapes=[pltpu.VMEM((s + 2 * POOL_PAD, LANES), _F32)],
        compiler_params=_params(2),
        name="pool",
    )(v_pool, pool_w_bd, pool_scale)


def _attn_kernel(qn_ref, kn_ref, vn_ref, qm_ref, km_ref, vm_ref, qf_ref, kf_ref, vf_ref, o_ref,
                 mask_ref, s_ref, p_ref, st_ref, num_far, m_far, l_far, num_mid, m_mid, l_mid):
    s_len = qn_ref.shape[1]
    far_dil = MID_DIL * MID_DIL
    operands = ((qn_ref, kn_ref, vn_ref), (qm_ref, km_ref, vm_ref), (qf_ref, kf_ref, vf_ref))
    first = lax.broadcasted_iota(jnp.int32, (Q_BLOCK, LANES), 1) < HEAD_DIM

    rr = lax.broadcasted_iota(jnp.int32, (Q_BLOCK, 2 * Q_BLOCK), 0)
    cc = lax.broadcasted_iota(jnp.int32, (Q_BLOCK, 2 * Q_BLOCK), 1)
    for idx in range(3):
        off = (idx - 2) * BAND_HALF
        mask_ref[idx] = (jnp.abs(cc - rr + off) <= BAND_HALF).astype(_F32)

    blocks = []
    for order, n_seq in ((2, far_dil), (1, MID_DIL), (0, 1)):
        seq = s_len // n_seq
        blocks += [(order, r, q0, seq) for r in range(n_seq) for q0 in range(0, seq, Q_BLOCK)]

    def key_window(q0, seq):
        kwin = min(2 * Q_BLOCK, seq)
        return min(max(q0 - BAND_HALF, 0), seq - kwin), kwin

    def scores(blk, slot):
        order, r, q0, seq = blk
        q_ref, k_ref, _ = operands[order]
        k0, kwin = key_window(q0, seq)
        q = q_ref[r, q0:q0 + Q_BLOCK, :]
        kblk = k_ref[r, k0:k0 + kwin, :]
        valid = mask_ref[(k0 - q0) // BAND_HALF + 2, :, 0:kwin] != 0.0
        zero = jnp.zeros_like(q)
        for h, qh in enumerate((jnp.where(first, q, zero), jnp.where(first, zero, q))):
            sc = lax.dot_general(qh, kblk, (((1,), (1,)), ((), ())),
                                 preferred_element_type=_F32)
            s_ref[slot, h, :, 0:kwin] = jnp.where(valid, sc, MASK_VALUE)

    def softmax(blk, slot):
        _, kwin = key_window(blk[2], blk[3])
        stats = []
        for h in range(2):
            sc = s_ref[slot, h, :, 0:kwin]
            m = jnp.max(sc, axis=-1, keepdims=True)
            p = jnp.exp2(sc - m)
            stats.append((m, jnp.sum(p, axis=-1, keepdims=True)))
            p_ref[slot, h, :, 0:kwin] = p.astype(_BF16)
        st_ref[slot, 0] = jnp.where(first, stats[0][0], stats[1][0])
        st_ref[slot, 1] = jnp.where(first, stats[0][1], stats[1][1])

    def values(blk, slot):
        order, r, q0, seq = blk
        v_ref = operands[order][2]
        k0, kwin = key_window(q0, seq)
        vblk = v_ref[r, k0:k0 + kwin, :]
        o = [jnp.dot(p_ref[slot, h, :, 0:kwin], vblk, preferred_element_type=_F32)
             for h in range(2)]
        num = jnp.where(first, o[0], o[1])
        m = st_ref[slot, 0]
        l = st_ref[slot, 1]
        rows = slice(q0, q0 + Q_BLOCK)
        if order == 2:
            out_rows = pl.ds(MID_DIL * q0 + r // MID_DIL, Q_BLOCK, stride=MID_DIL)
            num_far[r % MID_DIL, out_rows, :] = num
            m_far[r % MID_DIL, out_rows, :] = m
            l_far[r % MID_DIL, out_rows, :] = l
            return
        if order == 1:
            m_c, num_c, l_c = m_far[r, rows, :], num_far[r, rows, :], l_far[r, rows, :]
        else:
            m_c, num_c, l_c = m_mid[rows, :], num_mid[rows, :], l_mid[rows, :]
        m_new = jnp.maximum(m, m_c)
        a = jnp.exp2(m - m_new)
        a_c = jnp.exp2(m_c - m_new)
        num = a * num + a_c * num_c
        l = a * l + a_c * l_c
        if order == 1:
            out_rows = pl.ds(MID_DIL * q0 + r, Q_BLOCK, stride=MID_DIL)
            num_mid[out_rows, :] = num
            m_mid[out_rows, :] = m_new
            l_mid[out_rows, :] = l
        else:
            o_ref[rows, :] = (num / l).astype(o_ref.dtype)

    stages = (scores, softmax, values)
    for step in range(len(blocks) + len(stages) - 1):
        for depth, stage in enumerate(stages):
            i = step - depth
            if 0 <= i < len(blocks):
                stage(blocks[i], i % ATTN_SLOTS)


def _attention(nat, mid, far):
    b, s, _ = nat.shape
    n_pairs = ATTN_WIDTH // LANES
    far_dil = MID_DIL * MID_DIL

    def specs(n_seq):
        def tile(off):
            return pl.BlockSpec((None, n_seq, s // n_seq, LANES),
                                lambda i, j: (i, 0, 0, off + j))
        return [tile(0), tile(n_pairs), tile(2 * n_pairs)]

    return pl.pallas_call(
        _attn_kernel,
        grid=(b, n_pairs),
        in_specs=specs(1) + specs(MID_DIL) + specs(far_dil),
        out_specs=pl.BlockSpec((None, s, LANES), lambda i, j: (i, 0, j)),
        out_shape=jax.ShapeDtypeStruct((b, s, ATTN_WIDTH), _BF16),
        scratch_shapes=[pltpu.VMEM((3, Q_BLOCK, 2 * Q_BLOCK), _F32),
                        pltpu.VMEM((ATTN_SLOTS, 2, Q_BLOCK, 2 * Q_BLOCK), _F32),
                        pltpu.VMEM((ATTN_SLOTS, 2, Q_BLOCK, 2 * Q_BLOCK), _BF16),
                        pltpu.VMEM((ATTN_SLOTS, 2, Q_BLOCK, LANES), _F32)]
        + [pltpu.VMEM((MID_DIL, s // MID_DIL, LANES), _F32)] * 3
        + [pltpu.VMEM((s, LANES), _F32)] * 3,
        compiler_params=_params(2),
        name="dilated_attention",
    )(nat.reshape(b, 1, s, QKV_WIDTH), nat.reshape(b, 1, s, QKV_WIDTH),
      nat.reshape(b, 1, s, QKV_WIDTH), mid, mid, mid, far, far, far)


def _pool_block_diag(pool_w):
    g = POOL_GROUP_DIM
    tiles = POOL_WIDTH // LANES
    per_tile = LANES // g
    z = jnp.zeros((DEPTH, tiles, LANES, LANES), pool_w.dtype)
    for t in range(tiles):
        for h in range(per_tile):
            z = z.at[:, t, h * g:(h + 1) * g, h * g:(h + 1) * g].set(pool_w[:, per_tile * t + h])
    return z.astype(_BF16)


@jax.jit
def _forward(x, positions, ffn1_norm, ffn1_w_gate, ffn1_w_up, ffn1_w_down, mix_norm, w_in,
             pool_w, pool_scale, w_out, ffn2_norm, ffn2_w_gate, ffn2_w_up, ffn2_w_down,
             final_norm):
    b, s, d = x.shape
    n = b * s
    assert d == D_MODEL and s % ROW_TILE == 0 and s % (Q_BLOCK * MID_DIL * MID_DIL) == 0
    assert all(w // 2 == POOL_WINDOWS[0] * 2 ** i // 2 for i, w in enumerate(POOL_WINDOWS))

    pool_w_bd = _pool_block_diag(pool_w)
    g1 = ffn1_norm.reshape(DEPTH, 1, D_MODEL)
    g2 = ffn2_norm.reshape(DEPTH, 1, D_MODEL)
    gm = mix_norm.reshape(DEPTH, 1, D_MODEL)
    ps = pool_scale.reshape(DEPTH, 1, POOL_WIDTH)
    gf = final_norm.reshape(1, D_MODEL)

    inv_freq = ROPE_THETA ** (-jnp.arange(0, ROPE_DIM, 2, dtype=_F32) / ROPE_DIM)
    invf = jnp.tile(inv_freq, LANES // (ROPE_DIM // 2)).reshape(1, LANES)
    pos = positions.astype(_F32)[..., None]
    cos, sin_lo, sin_hi = _rope_tables(pos, invf)

    xs = x.reshape(n, D_MODEL)
    for layer in range(DEPTH):
        xs = _ffn(xs, g1, ffn1_w_gate, ffn1_w_up, ffn1_w_down, gf, layer, False)
        v_pool, nat, mid, far = _inproj(xs.reshape(b, s, D_MODEL), gm, w_in, cos, sin_lo,
                                        sin_hi, layer)
        y_pool = _pool(v_pool, pool_w_bd, ps, layer)
        y_attn = _attention(nat, mid, far)
        mixer = (y_pool.reshape(n, POOL_WIDTH), y_attn.reshape(n, ATTN_WIDTH), w_out)
        xs = _ffn(xs, g2, ffn2_w_gate, ffn2_w_up, ffn2_w_down, gf, layer, layer == DEPTH - 1,
                  mixer)
    return xs.reshape(b, s, D_MODEL)


def kernel(x, positions, ffn1_norm, ffn1_w_gate, ffn1_w_up, ffn1_w_down, mix_norm, w_in,
           pool_w, pool_scale, w_out, ffn2_norm, ffn2_w_gate, ffn2_w_up, ffn2_w_down,
           final_norm):
    return _forward(x, positions, ffn1_norm, ffn1_w_gate, ffn1_w_up, ffn1_w_down, mix_norm,
                    w_in, pool_w, pool_scale, w_out, ffn2_norm, ffn2_w_gate, ffn2_w_up,
                    ffn2_w_down, final_norm)
```

```python
import functools

import jax
import jax.numpy as jnp
from jax import lax
from jax.experimental import pallas as pl
from jax.experimental.pallas import tpu as pltpu

D_MODEL = 1024
DEPTH = 4
POOL_WIDTH = 256
POOL_WINDOWS = (2, 4, 8, 16)
POOL_GROUP_DIM = 64
HEAD_DIM = 64
ATTN_WIDTH = 768
DILATED_PATTERNS = ((128, 1), (512, 4), (2048, 16))
ROPE_THETA = 500000.0
ROPE_DIM = 16
D_FF = 2816
IN_PROJ_WIDTH = 2560
NORM_EPS = 1e-6
MASK_VALUE = -1e30
LOG2_E = 1.4426950408889634

LANES = 128
MXU_WIDTH = 256
ROW_TILE = 512
FF_CHUNK = MXU_WIDTH
N_FF_CHUNKS = D_FF // FF_CHUNK
Q_BLOCK = 128
BAND_HALF = 64
ATTN_SLOTS = 3
MID_DIL = 4
QKV_WIDTH = 3 * ATTN_WIDTH
POOL_PAD = 16
VMEM_LIMIT = 60 * 1024 * 1024

_F32 = jnp.float32
_BF16 = jnp.bfloat16

assert all(w // (2 * d) == BAND_HALF for w, d in DILATED_PATTERNS)
assert [d for _, d in DILATED_PATTERNS] == [1, MID_DIL, MID_DIL * MID_DIL]


def _params(n_grid_dims):
    return pltpu.CompilerParams(
        dimension_semantics=("arbitrary",) * n_grid_dims,
        vmem_limit_bytes=VMEM_LIMIT)


def _resident(block_shape, index_map):
    return pl.BlockSpec(block_shape, index_map, pipeline_mode=pl.Buffered(1))


def _rmsnorm(x, g):
    y = x * lax.rsqrt(jnp.mean(x * x, axis=-1, keepdims=True) + NORM_EPS)
    return y * g


def _ffn_kernel(*refs, mix_in, tail):
    refs = list(refs)
    x_ref = refs.pop(0)
    if mix_in:
        yp_ref, ya_ref, wo_ref = refs[:3]
        del refs[:3]
    g_ref, wg_ref, wu_ref, wd_ref = refs[:4]
    del refs[:4]
    tg_ref = refs.pop(0) if tail else None
    o_ref = refs.pop(0)
    hn_ref = refs.pop(0) if tail == "emit_norm" else None
    h_ref, acc_ref = refs

    x = x_ref[...]
    if mix_in:
        x = x + jnp.dot(yp_ref[...], wo_ref[0:POOL_WIDTH, :].astype(_BF16),
                        preferred_element_type=_F32)
        x = x + jnp.dot(ya_ref[...], wo_ref[POOL_WIDTH:D_MODEL, :].astype(_BF16),
                        preferred_element_type=_F32)
        o_ref[...] = x
    h_ref[...] = _rmsnorm(x, g_ref[...]).astype(_BF16)

    for c in range(N_FF_CHUNKS):
        cols = slice(c * FF_CHUNK, (c + 1) * FF_CHUNK)
        h = h_ref[...]
        gate = jnp.dot(h, wg_ref[:, cols].astype(_BF16), preferred_element_type=_F32)
        up = jnp.dot(h, wu_ref[:, cols].astype(_BF16), preferred_element_type=_F32)
        act = ((gate * jax.nn.sigmoid(gate)) * up).astype(_BF16)
        down = jnp.dot(act, wd_ref[cols, :].astype(_BF16), preferred_element_type=_F32)
        if c == 0:
            acc_ref[...] = down
        elif c < N_FF_CHUNKS - 1:
            acc_ref[...] += down
    base = o_ref[...] if mix_in else x_ref[...]
    y = base + 0.5 * (acc_ref[...] + down)
    if tail == "final_norm":
        y = _rmsnorm(y, tg_ref[...])
    o_ref[...] = y
    if tail == "emit_norm":
        hn_ref[...] = _rmsnorm(y, tg_ref[...]).astype(_BF16)


def _ffn(x, gains, wg, wu, wd, layer, mixer=None, tail=None, tail_gain=None):
    n = x.shape[0]
    rows = lambda width: pl.BlockSpec((ROW_TILE, width), lambda i: (i, 0))
    operands = [x]
    in_specs = [rows(D_MODEL)]
    if mixer is not None:
        operands += list(mixer)
        in_specs += [rows(POOL_WIDTH), rows(ATTN_WIDTH),
                     _resident((None, D_MODEL, D_MODEL), lambda i: (layer, 0, 0))]
    operands += [gains, wg, wu, wd]
    in_specs += [
        pl.BlockSpec((None, 1, D_MODEL), lambda i: (layer, 0, 0)),
        _resident((None, D_MODEL, D_FF), lambda i: (layer, 0, 0)),
        _resident((None, D_MODEL, D_FF), lambda i: (layer, 0, 0)),
        _resident((None, D_FF, D_MODEL), lambda i: (layer, 0, 0)),
    ]
    out_specs = [rows(D_MODEL)]
    out_shape = [jax.ShapeDtypeStruct((n, D_MODEL), _F32)]
    if tail is not None:
        gain_stack, gain_index = tail_gain
        operands.append(gain_stack)
        in_specs.append(pl.BlockSpec((None, 1, D_MODEL), lambda i: (gain_index, 0, 0)))
    if tail == "emit_norm":
        out_specs.append(rows(D_MODEL))
        out_shape.append(jax.ShapeDtypeStruct((n, D_MODEL), _BF16))
    return pl.pallas_call(
        functools.partial(_ffn_kernel, mix_in=mixer is not None, tail=tail),
        grid=(n // ROW_TILE,),
        in_specs=in_specs,
        out_specs=out_specs,
        out_shape=out_shape,
        scratch_shapes=[pltpu.VMEM((ROW_TILE, D_MODEL), _BF16),
                        pltpu.VMEM((ROW_TILE, D_MODEL), _F32)],
        compiler_params=_params(1),
        name="ffn",
    )(*operands)


def _rope_table_kernel(pos_ref, invf_ref, cos_ref, sin_ref):
    ang = pos_ref[...] * invf_ref[...]
    c = jnp.cos(ang)
    s = jnp.sin(ang)
    d = lax.broadcasted_iota(jnp.int32, ang.shape, 1) & (HEAD_DIM - 1)
    cos_ref[...] = jnp.where(d < ROPE_DIM, c, 1.0)
    sin_ref[...] = jnp.where(d < ROPE_DIM // 2, -s, jnp.where(d < ROPE_DIM, s, 0.0))


def _rope_tables(pos, invf):
    b, s, _ = pos.shape
    out = jax.ShapeDtypeStruct((b, s, LANES), _F32)
    spec = pl.BlockSpec((None, s, LANES), lambda i: (i, 0, 0))
    return pl.pallas_call(
        _rope_table_kernel,
        grid=(b,),
        in_specs=[pl.BlockSpec((None, s, 1), lambda i: (i, 0, 0)),
                  pl.BlockSpec((1, LANES), lambda i: (0, 0))],
        out_specs=[spec, spec],
        out_shape=[out, out],
        compiler_params=_params(1),
        name="rope_tables",
    )(pos, invf)


def _inproj_kernel(h_ref, w_ref, cos_ref, sin_ref,
                   vpool_ref, nat_ref, mid_ref, far_ref, slab_ref, midslab_ref):
    h = h_ref[...]
    cos = cos_ref[...]
    sin = sin_ref[...]
    half_rope = ROPE_DIM // 2
    n_heads_tiles = ATTN_WIDTH // LANES
    on_t1 = (lax.broadcasted_iota(jnp.int32, cos.shape, 1) & (HEAD_DIM - 1)) < half_rope

    def rope(t):
        partner = jnp.where(on_t1, pltpu.roll(t, LANES - half_rope, 1), pltpu.roll(t, half_rope, 1))
        return t * cos + partner * sin

    for c in range(IN_PROJ_WIDTH // MXU_WIDTH):
        cols = slice(c * MXU_WIDTH, (c + 1) * MXU_WIDTH)
        part = jnp.dot(h, w_ref[:, cols].astype(_BF16), preferred_element_type=_F32)
        if c * MXU_WIDTH < POOL_WIDTH:
            vpool_ref[:, cols] = part
            continue
        for half in range(MXU_WIDTH // LANES):
            j = (c * MXU_WIDTH - POOL_WIDTH) // LANES + half
            t = part[:, half * LANES:(half + 1) * LANES]
            if j < n_heads_tiles:
                t = rope(t) * (LOG2_E / (HEAD_DIM ** 0.5))
            elif j < 2 * n_heads_tiles:
                t = rope(t)
            slab_ref[j] = t
            nat_ref[:, j * LANES:(j + 1) * LANES] = t.astype(_BF16)

    far_dil = MID_DIL * MID_DIL
    for j in range(QKV_WIDTH // LANES):
        lanes = slice(j * LANES, (j + 1) * LANES)
        for r in range(MID_DIL):
            rows = pl.ds(r, ROW_TILE // MID_DIL, stride=MID_DIL)
            picked = slab_ref[j, rows, :]
            mid_ref[r, :, lanes] = picked.astype(_BF16)
            midslab_ref[j, r] = picked
        for r in range(far_dil):
            rows = pl.ds(r // MID_DIL, ROW_TILE // far_dil, stride=MID_DIL)
            far_ref[r, :, lanes] = midslab_ref[j, r % MID_DIL, rows, :].astype(_BF16)


def _inproj(h, w_in, cos, sin, layer):
    b, s, _ = h.shape
    tiles = s // ROW_TILE
    far_dil = MID_DIL * MID_DIL
    table = pl.BlockSpec((None, ROW_TILE, LANES), lambda i, t: (i, t, 0))
    return pl.pallas_call(
        _inproj_kernel,
        grid=(b, tiles),
        in_specs=[
            pl.BlockSpec((None, ROW_TILE, D_MODEL), lambda i, t: (i, t, 0)),
            _resident((None, D_MODEL, IN_PROJ_WIDTH), lambda i, t: (layer, 0, 0)),
            table, table,
        ],
        out_specs=[
            pl.BlockSpec((None, ROW_TILE, POOL_WIDTH), lambda i, t: (i, t, 0)),
            pl.BlockSpec((None, ROW_TILE, QKV_WIDTH), lambda i, t: (i, t, 0)),
            pl.BlockSpec((None, MID_DIL, ROW_TILE // MID_DIL, QKV_WIDTH),
                         lambda i, t: (i, 0, t, 0)),
            pl.BlockSpec((None, far_dil, ROW_TILE // far_dil, QKV_WIDTH),
                         lambda i, t: (i, 0, t, 0)),
        ],
        out_shape=[
            jax.ShapeDtypeStruct((b, s, POOL_WIDTH), _F32),
            jax.ShapeDtypeStruct((b, s, QKV_WIDTH), _BF16),
            jax.ShapeDtypeStruct((b, MID_DIL, s // MID_DIL, QKV_WIDTH), _BF16),
            jax.ShapeDtypeStruct((b, far_dil, s // far_dil, QKV_WIDTH), _BF16),
        ],
        scratch_shapes=[
            pltpu.VMEM((QKV_WIDTH // LANES, ROW_TILE, LANES), _F32),
            pltpu.VMEM((QKV_WIDTH // LANES, MID_DIL, ROW_TILE // MID_DIL, LANES), _F32)],
        compiler_params=_params(2),
        name="inproj",
    )(h, w_in, cos, sin)


def _pool_kernel(v_ref, w_ref, scale_ref, o_ref, pad_ref, cnt_ref):
    s = v_ref.shape[0]
    tile = pl.program_id(1)
    v = v_ref[...]

    @pl.when(pl.program_id(0) == 0)
    def _():
        lane_first = lax.broadcasted_iota(jnp.int32, v.shape, 1) < POOL_GROUP_DIM
        half_w = jnp.where(tile == 0, jnp.where(lane_first, 1, 2), jnp.where(lane_first, 4, 8))
        pos = lax.broadcasted_iota(jnp.int32, v.shape, 0)
        lo = jnp.maximum(pos - half_w, 0)
        hi = jnp.minimum(pos + half_w - 1, s - 1)
        cnt_ref[tile] = (hi - lo + 1).astype(_F32)

    pad_ref[...] = jnp.zeros_like(pad_ref)
    pad_ref[POOL_PAD:POOL_PAD + s, :] = v
    x = pad_ref[...]
    rows = x.shape[0]

    def shifted(t, k):
        return pltpu.roll(t, k % rows, 0)

    w2 = x + shifted(x, 1)
    w4 = shifted(w2, 1) + shifted(w2, -1)
    w8 = shifted(w4, 2) + shifted(w4, -2)
    w16 = shifted(w8, 4) + shifted(w8, -4)
    first_group = lax.broadcasted_iota(jnp.int32, x.shape, 1) < POOL_GROUP_DIM
    small = jnp.where(first_group, w2, w4)
    large = jnp.where(first_group, w8, w16)
    wsum = jnp.where(tile == 0, small, large)[POOL_PAD:POOL_PAD + s, :]

    diff = wsum / cnt_ref[tile] - v
    y = jnp.dot(diff.astype(_BF16), w_ref[...], preferred_element_type=_F32)
    o_ref[...] = (y * scale_ref[...]).astype(o_ref.dtype)


def _pool(v_pool, pool_w_bd, pool_scale, layer):
    b, s, _ = v_pool.shape
    n_tiles = POOL_WIDTH // LANES
    return pl.pallas_call(
        _pool_kernel,
        grid=(b, n_tiles),
        in_specs=[
            pl.BlockSpec((None, s, LANES), lambda i, j: (i, 0, j)),
            pl.BlockSpec((None, None, LANES, LANES), lambda i, j: (layer, j, 0, 0)),
            pl.BlockSpec((None, 1, LANES), lambda i, j: (layer, 0, j)),
        ],
        out_specs=pl.BlockSpec((None, s, LANES), lambda i, j: (i, 0, j)),
        out_shape=jax.ShapeDtypeStruct((b, s, POOL_WIDTH), _BF16),
        scratch_shapes=[pltpu.VMEM((s + 2 * POOL_PAD, LANES), _F32),
                        pltpu.VMEM((n_tiles, s, LANES), _F32)],
        compiler_params=_params(2),
        name="pool",
    )(v_pool, pool_w_bd, pool_scale)


def _attn_kernel(qn_ref, kn_ref, vn_ref, qm_ref, km_ref, vm_ref, qf_ref, kf_ref, vf_ref, o_ref,
                 mask_ref, s_ref, p_ref, st_ref, num_far, m_far, l_far, num_mid, m_mid, l_mid):
    s_len = qn_ref.shape[1]
    far_dil = MID_DIL * MID_DIL
    operands = ((qn_ref, kn_ref, vn_ref), (qm_ref, km_ref, vm_ref), (qf_ref, kf_ref, vf_ref))
    first = lax.broadcasted_iota(jnp.int32, (Q_BLOCK, LANES), 1) < HEAD_DIM

    rr = lax.broadcasted_iota(jnp.int32, (Q_BLOCK, 2 * Q_BLOCK), 0)
    cc = lax.broadcasted_iota(jnp.int32, (Q_BLOCK, 2 * Q_BLOCK), 1)
    for idx in range(3):
        off = (idx - 2) * BAND_HALF
        mask_ref[idx] = jnp.where(jnp.abs(cc - rr + off) <= BAND_HALF, jnp.inf, MASK_VALUE)

    blocks = []
    for order, n_seq in ((2, far_dil), (1, MID_DIL), (0, 1)):
        seq = s_len // n_seq
        blocks += [(order, r, q0, seq) for r in range(n_seq) for q0 in range(0, seq, Q_BLOCK)]

    def key_window(q0, seq):
        kwin = min(2 * Q_BLOCK, seq)
        return min(max(q0 - BAND_HALF, 0), seq - kwin), kwin

    def scores(blk, slot):
        order, r, q0, seq = blk
        q_ref, k_ref, _ = operands[order]
        k0, kwin = key_window(q0, seq)
        q = q_ref[r, q0:q0 + Q_BLOCK, :]
        kblk = k_ref[r, k0:k0 + kwin, :]
        cap = mask_ref[(k0 - q0) // BAND_HALF + 2, :, 0:kwin]
        zero = jnp.zeros_like(q)
        for h, qh in enumerate((jnp.where(first, q, zero), jnp.where(first, zero, q))):
            sc = lax.dot_general(qh, kblk, (((1,), (1,)), ((), ())),
                                 preferred_element_type=_F32)
            s_ref[slot, h, :, 0:kwin] = jnp.minimum(sc, cap)

    def softmax(blk, slot):
        _, kwin = key_window(blk[2], blk[3])
        stats = []
        for h in range(2):
            sc = s_ref[slot, h, :, 0:kwin]
            m = jnp.max(sc, axis=-1, keepdims=True)
            p = jnp.exp2(sc - m)
            stats.append((m, jnp.sum(p, axis=-1, keepdims=True)))
            p_ref[slot, h, :, 0:kwin] = p.astype(_BF16)
        st_ref[slot, 0] = jnp.where(first, stats[0][0], stats[1][0])
        st_ref[slot, 1] = jnp.where(first, stats[0][1], stats[1][1])

    def values(blk, slot):
        order, r, q0, seq = blk
        v_ref = operands[order][2]
        k0, kwin = key_window(q0, seq)
        vblk = v_ref[r, k0:k0 + kwin, :]
        o = [jnp.dot(p_ref[slot, h, :, 0:kwin], vblk, preferred_element_type=_F32)
             for h in range(2)]
        num = jnp.where(first, o[0], o[1])
        m = st_ref[slot, 0]
        l = st_ref[slot, 1]
        rows = slice(q0, q0 + Q_BLOCK)
        if order == 2:
            out_rows = pl.ds(MID_DIL * q0 + r // MID_DIL, Q_BLOCK, stride=MID_DIL)
            num_far[r % MID_DIL, out_rows, :] = num
            m_far[r % MID_DIL, out_rows, :] = m
            l_far[r % MID_DIL, out_rows, :] = l
            return
        if order == 1:
            m_c, num_c, l_c = m_far[r, rows, :], num_far[r, rows, :], l_far[r, rows, :]
        else:
            m_c, num_c, l_c = m_mid[rows, :], num_mid[rows, :], l_mid[rows, :]
        m_new = jnp.maximum(m, m_c)
        a = jnp.exp2(m - m_new)
        a_c = jnp.exp2(m_c - m_new)
        num = a * num + a_c * num_c
        l = a * l + a_c * l_c
        if order == 1:
            out_rows = pl.ds(MID_DIL * q0 + r, Q_BLOCK, stride=MID_DIL)
            num_mid[out_rows, :] = num
            m_mid[out_rows, :] = m_new
            l_mid[out_rows, :] = l
        else:
            o_ref[rows, :] = (num / l).astype(o_ref.dtype)

    stages = (scores, softmax, values)
    for step in range(len(blocks) + len(stages) - 1):
        for depth, stage in enumerate(stages):
            i = step - depth
            if 0 <= i < len(blocks):
                stage(blocks[i], i % ATTN_SLOTS)


def _attention(nat, mid, far):
    b, s, _ = nat.shape
    n_pairs = ATTN_WIDTH // LANES
    far_dil = MID_DIL * MID_DIL

    def specs(n_seq):
        def tile(off):
            return pl.BlockSpec((None, n_seq, s // n_seq, LANES),
                                lambda i, j: (i, 0, 0, off + j))
        return [tile(0), tile(n_pairs), tile(2 * n_pairs)]

    return pl.pallas_call(
        _attn_kernel,
        grid=(b, n_pairs),
        in_specs=specs(1) + specs(MID_DIL) + specs(far_dil),
        out_specs=pl.BlockSpec((None, s, LANES), lambda i, j: (i, 0, j)),
        out_shape=jax.ShapeDtypeStruct((b, s, ATTN_WIDTH), _BF16),
        scratch_shapes=[pltpu.VMEM((3, Q_BLOCK, 2 * Q_BLOCK), _F32),
                        pltpu.VMEM((ATTN_SLOTS, 2, Q_BLOCK, 2 * Q_BLOCK), _F32),
                        pltpu.VMEM((ATTN_SLOTS, 2, Q_BLOCK, 2 * Q_BLOCK), _BF16),
                        pltpu.VMEM((ATTN_SLOTS, 2, Q_BLOCK, LANES), _F32)]
        + [pltpu.VMEM((MID_DIL, s // MID_DIL, LANES), _F32)] * 3
        + [pltpu.VMEM((s, LANES), _F32)] * 3,
        compiler_params=_params(2),
        name="dilated_attention",
    )(nat.reshape(b, 1, s, QKV_WIDTH), nat.reshape(b, 1, s, QKV_WIDTH),
      nat.reshape(b, 1, s, QKV_WIDTH), mid, mid, mid, far, far, far)


def _pool_block_diag(pool_w):
    g = POOL_GROUP_DIM
    tiles = POOL_WIDTH // LANES
    per_tile = LANES // g
    z = jnp.zeros((DEPTH, tiles, LANES, LANES), pool_w.dtype)
    for t in range(tiles):
        for h in range(per_tile):
            z = z.at[:, t, h * g:(h + 1) * g, h * g:(h + 1) * g].set(pool_w[:, per_tile * t + h])
    return z.astype(_BF16)


@jax.jit
def _forward(x, positions, ffn1_norm, ffn1_w_gate, ffn1_w_up, ffn1_w_down, mix_norm, w_in,
             pool_w, pool_scale, w_out, ffn2_norm, ffn2_w_gate, ffn2_w_up, ffn2_w_down,
             final_norm):
    b, s, d = x.shape
    n = b * s
    assert d == D_MODEL and s % ROW_TILE == 0 and s % (Q_BLOCK * MID_DIL * MID_DIL) == 0
    assert all(w // 2 == POOL_WINDOWS[0] * 2 ** i // 2 for i, w in enumerate(POOL_WINDOWS))

    pool_w_bd = _pool_block_diag(pool_w)
    g1 = ffn1_norm.reshape(DEPTH, 1, D_MODEL)
    g2 = ffn2_norm.reshape(DEPTH, 1, D_MODEL)
    gm = mix_norm.reshape(DEPTH, 1, D_MODEL)
    ps = pool_scale.reshape(DEPTH, 1, POOL_WIDTH)
    gf = final_norm.reshape(1, 1, D_MODEL)

    inv_freq = ROPE_THETA ** (-jnp.arange(0, ROPE_DIM, 2, dtype=_F32) / ROPE_DIM)
    invf = jnp.tile(inv_freq, LANES // (ROPE_DIM // 2)).reshape(1, LANES)
    pos = positions.astype(_F32)[..., None]
    cos, sin = _rope_tables(pos, invf)

    xs = x.reshape(n, D_MODEL)
    for layer in range(DEPTH):
        xs, h_mix = _ffn(xs, g1, ffn1_w_gate, ffn1_w_up, ffn1_w_down, layer,
                         tail="emit_norm", tail_gain=(gm, layer))
        v_pool, nat, mid, far = _inproj(h_mix.reshape(b, s, D_MODEL), w_in, cos, sin, layer)
        y_pool = _pool(v_pool, pool_w_bd, ps, layer)
        y_attn = _attention(nat, mid, far)
        mixer = (y_pool.reshape(n, POOL_WIDTH), y_attn.reshape(n, ATTN_WIDTH), w_out)
        last = layer == DEPTH - 1
        xs, = _ffn(xs, g2, ffn2_w_gate, ffn2_w_up, ffn2_w_down, layer, mixer=mixer,
                   tail="final_norm" if last else None, tail_gain=(gf, 0) if last else None)
    return xs.reshape(b, s, D_MODEL)


def kernel(x, positions, ffn1_norm, ffn1_w_gate, ffn1_w_up, ffn1_w_down, mix_norm, w_in,
           pool_w, pool_scale, w_out, ffn2_norm, ffn2_w_gate, ffn2_w_up, ffn2_w_down,
           final_norm):
    return _forward(x, positions, ffn1_norm, ffn1_w_gate, ffn1_w_up, ffn1_w_down, mix_norm,
                    w_in, pool_w, pool_scale, w_out, ffn2_norm, ffn2_w_gate, ffn2_w_up,
                    ffn2_w_down, final_norm)
```

```python
import functools

import jax
import jax.numpy as jnp
from jax import lax
from jax.experimental import pallas as pl
from jax.experimental.pallas import tpu as pltpu

D_MODEL = 1024
DEPTH = 4
POOL_WIDTH = 256
POOL_WINDOWS = (2, 4, 8, 16)
POOL_GROUP_DIM = 64
HEAD_DIM = 64
ATTN_WIDTH = 768
DILATED_PATTERNS = ((128, 1), (512, 4), (2048, 16))
ROPE_THETA = 500000.0
ROPE_DIM = 16
D_FF = 2816
IN_PROJ_WIDTH = 2560
NORM_EPS = 1e-6
MASK_VALUE = -1e30
LOG2_E = 1.4426950408889634

LANES = 128
MXU_WIDTH = 256
ROW_TILE = 512
FF_CHUNK = MXU_WIDTH
N_FF_CHUNKS = D_FF // FF_CHUNK
Q_BLOCK = 128
BAND_HALF = 64
ATTN_SLOTS = 3
MID_DIL = 4
QKV_WIDTH = 3 * ATTN_WIDTH
POOL_PAD = 16
VMEM_LIMIT = 60 * 1024 * 1024

_F32 = jnp.float32
_BF16 = jnp.bfloat16

assert all(w // (2 * d) == BAND_HALF for w, d in DILATED_PATTERNS)
assert [d for _, d in DILATED_PATTERNS] == [1, MID_DIL, MID_DIL * MID_DIL]


def _params(n_grid_dims):
    return pltpu.CompilerParams(
        dimension_semantics=("arbitrary",) * n_grid_dims,
        vmem_limit_bytes=VMEM_LIMIT)


def _resident(block_shape, index_map):
    return pl.BlockSpec(block_shape, index_map, pipeline_mode=pl.Buffered(1))


def _rmsnorm(x, g):
    y = x * lax.rsqrt(jnp.mean(x * x, axis=-1, keepdims=True) + NORM_EPS)
    return y * g


def _ffn_kernel(*refs, mix_in, tail):
    refs = list(refs)
    x_ref = refs.pop(0)
    if mix_in:
        yp_ref, ya_ref, wo_ref = refs[:3]
        del refs[:3]
    g_ref, wg_ref, wu_ref, wd_ref = refs[:4]
    del refs[:4]
    tg_ref = refs.pop(0) if tail else None
    o_ref = refs.pop(0)
    hn_ref = refs.pop(0) if tail == "emit_norm" else None
    h_ref, acc_ref = refs

    x = x_ref[...]
    if mix_in:
        x = x + jnp.dot(yp_ref[...], wo_ref[0:POOL_WIDTH, :].astype(_BF16),
                        preferred_element_type=_F32)
        x = x + jnp.dot(ya_ref[...], wo_ref[POOL_WIDTH:D_MODEL, :].astype(_BF16),
                        preferred_element_type=_F32)
        o_ref[...] = x
    h_ref[...] = _rmsnorm(x, g_ref[...]).astype(_BF16)

    for c in range(N_FF_CHUNKS):
        cols = slice(c * FF_CHUNK, (c + 1) * FF_CHUNK)
        h = h_ref[...]
        gate = jnp.dot(h, wg_ref[:, cols].astype(_BF16), preferred_element_type=_F32)
        up = jnp.dot(h, wu_ref[:, cols].astype(_BF16), preferred_element_type=_F32)
        act = ((gate * jax.nn.sigmoid(gate)) * up).astype(_BF16)
        down = jnp.dot(act, wd_ref[cols, :].astype(_BF16), preferred_element_type=_F32)
        if c == 0:
            acc_ref[...] = down
        elif c < N_FF_CHUNKS - 1:
            acc_ref[...] += down
    base = o_ref[...] if mix_in else x_ref[...]
    y = base + 0.5 * (acc_ref[...] + down)
    if tail == "final_norm":
        y = _rmsnorm(y, tg_ref[...])
    o_ref[...] = y
    if tail == "emit_norm":
        hn_ref[...] = _rmsnorm(y, tg_ref[...]).astype(_BF16)


def _ffn(x, gains, wg, wu, wd, layer, mixer=None, tail=None, tail_gain=None):
    n = x.shape[0]
    rows = lambda width: pl.BlockSpec((ROW_TILE, width), lambda i: (i, 0))
    operands = [x]
    in_specs = [rows(D_MODEL)]
    if mixer is not None:
        operands += list(mixer)
        in_specs += [rows(POOL_WIDTH), rows(ATTN_WIDTH),
                     _resident((None, D_MODEL, D_MODEL), lambda i: (layer, 0, 0))]
    operands += [gains, wg, wu, wd]
    in_specs += [
        pl.BlockSpec((None, 1, D_MODEL), lambda i: (layer, 0, 0)),
        _resident((None, D_MODEL, D_FF), lambda i: (layer, 0, 0)),
        _resident((None, D_MODEL, D_FF), lambda i: (layer, 0, 0)),
        _resident((None, D_FF, D_MODEL), lambda i: (layer, 0, 0)),
    ]
    out_specs = [rows(D_MODEL)]
    out_shape = [jax.ShapeDtypeStruct((n, D_MODEL), _F32)]
    if tail is not None:
        gain_stack, gain_index = tail_gain
        operands.append(gain_stack)
        in_specs.append(pl.BlockSpec((None, 1, D_MODEL), lambda i: (gain_index, 0, 0)))
    if tail == "emit_norm":
        out_specs.append(rows(D_MODEL))
        out_shape.append(jax.ShapeDtypeStruct((n, D_MODEL), _BF16))
    return pl.pallas_call(
        functools.partial(_ffn_kernel, mix_in=mixer is not None, tail=tail),
        grid=(n // ROW_TILE,),
        in_specs=in_specs,
        out_specs=out_specs,
        out_shape=out_shape,
        scratch_shapes=[pltpu.VMEM((ROW_TILE, D_MODEL), _BF16),
                        pltpu.VMEM((ROW_TILE, D_MODEL), _F32)],
        compiler_params=_params(1),
        name="ffn",
    )(*operands)


def _rope_table_kernel(pos_ref, invf_ref, cos_ref, sin_ref):
    ang = pos_ref[...] * invf_ref[...]
    c = jnp.cos(ang)
    s = jnp.sin(ang)
    d = lax.broadcasted_iota(jnp.int32, ang.shape, 1) & (HEAD_DIM - 1)
    cos_ref[...] = jnp.where(d < ROPE_DIM, c, 1.0)
    sin_ref[...] = jnp.where(d < ROPE_DIM // 2, -s, jnp.where(d < ROPE_DIM, s, 0.0))


def _rope_tables(pos, invf):
    b, s, _ = pos.shape
    out = jax.ShapeDtypeStruct((b, s, LANES), _F32)
    spec = pl.BlockSpec((None, s, LANES), lambda i: (i, 0, 0))
    return pl.pallas_call(
        _rope_table_kernel,
        grid=(b,),
        in_specs=[pl.BlockSpec((None, s, 1), lambda i: (i, 0, 0)),
                  pl.BlockSpec((1, LANES), lambda i: (0, 0))],
        out_specs=[spec, spec],
        out_shape=[out, out],
        compiler_params=_params(1),
        name="rope_tables",
    )(pos, invf)


def _pool_head_group(h_ref, hprev_ref, hnext_ref, w_ref, pw_ref, ps_ref, ypool_ref,
                     pad_ref, cnt_ref):
    tile = pl.program_id(1)
    last_tile = pl.num_programs(1) - 1
    g = POOL_GROUP_DIM

    @pl.when(pl.program_id(0) == 0)
    def _():
        col = lax.broadcasted_iota(jnp.int32, (ROW_TILE, POOL_WIDTH), 1)
        half_w = jnp.where(col < g, POOL_WINDOWS[0] // 2,
                           jnp.where(col < 2 * g, POOL_WINDOWS[1] // 2,
                                     jnp.where(col < 3 * g, POOL_WINDOWS[2] // 2,
                                               POOL_WINDOWS[3] // 2)))
        pos = lax.broadcasted_iota(jnp.int32, (ROW_TILE, POOL_WIDTH), 0) + tile * ROW_TILE
        seq_len = pl.num_programs(1) * ROW_TILE
        lo = jnp.maximum(pos - half_w, 0)
        hi = jnp.minimum(pos + half_w - 1, seq_len - 1)
        cnt_ref[tile] = (hi - lo + 1).astype(_F32)

    w_pool = w_ref[:, 0:POOL_WIDTH].astype(_BF16)
    v = jnp.dot(h_ref[...], w_pool, preferred_element_type=_F32)
    v_prev = jnp.dot(hprev_ref[...], w_pool, preferred_element_type=_F32)
    v_next = jnp.dot(hnext_ref[...], w_pool, preferred_element_type=_F32)
    pad_ref[0:POOL_PAD, :] = jnp.where(tile > 0, v_prev, 0.0)
    pad_ref[POOL_PAD:POOL_PAD + ROW_TILE, :] = v
    pad_ref[POOL_PAD + ROW_TILE:, :] = jnp.where(tile < last_tile, v_next, 0.0)
    x = pad_ref[...]
    rows = x.shape[0]

    def shifted(t, k):
        return pltpu.roll(t, k % rows, 0)

    w2 = x + shifted(x, 1)
    w4 = shifted(w2, 1) + shifted(w2, -1)
    w8 = shifted(w4, 2) + shifted(w4, -2)
    w16 = shifted(w8, 4) + shifted(w8, -4)
    lane = lax.broadcasted_iota(jnp.int32, x.shape, 1)
    wsum = jnp.where(lane < g, w2, jnp.where(lane < 2 * g, w4, jnp.where(lane < 3 * g, w8, w16)))
    wsum = wsum[POOL_PAD:POOL_PAD + ROW_TILE, :]
    diff = wsum / cnt_ref[tile] - v
    y = jnp.dot(diff.astype(_BF16), pw_ref[...], preferred_element_type=_F32)
    ypool_ref[...] = (y * ps_ref[...]).astype(ypool_ref.dtype)


def _inproj_kernel(h_ref, hprev_ref, hnext_ref, w_ref, pw_ref, ps_ref, cos_ref, sin_ref,
                   ypool_ref, nat_ref, mid_ref, far_ref,
                   slab_ref, midslab_ref, pad_ref, cnt_ref):
    _pool_head_group(h_ref, hprev_ref, hnext_ref, w_ref, pw_ref, ps_ref, ypool_ref,
                     pad_ref, cnt_ref)
    h = h_ref[...]
    cos = cos_ref[...]
    sin = sin_ref[...]
    half_rope = ROPE_DIM // 2
    n_heads_tiles = ATTN_WIDTH // LANES
    on_t1 = (lax.broadcasted_iota(jnp.int32, cos.shape, 1) & (HEAD_DIM - 1)) < half_rope

    def rope(t):
        partner = jnp.where(on_t1, pltpu.roll(t, LANES - half_rope, 1), pltpu.roll(t, half_rope, 1))
        return t * cos + partner * sin

    for c in range(POOL_WIDTH // MXU_WIDTH, IN_PROJ_WIDTH // MXU_WIDTH):
        cols = slice(c * MXU_WIDTH, (c + 1) * MXU_WIDTH)
        part = jnp.dot(h, w_ref[:, cols].astype(_BF16), preferred_element_type=_F32)
        for half in range(MXU_WIDTH // LANES):
            j = (c * MXU_WIDTH - POOL_WIDTH) // LANES + half
            t = part[:, half * LANES:(half + 1) * LANES]
            if j < n_heads_tiles:
                t = rope(t) * (LOG2_E / (HEAD_DIM ** 0.5))
            elif j < 2 * n_heads_tiles:
                t = rope(t)
            slab_ref[j] = t
            nat_ref[:, j * LANES:(j + 1) * LANES] = t.astype(_BF16)

    far_dil = MID_DIL * MID_DIL
    for j in range(QKV_WIDTH // LANES):
        lanes = slice(j * LANES, (j + 1) * LANES)
        for r in range(MID_DIL):
            rows = pl.ds(r, ROW_TILE // MID_DIL, stride=MID_DIL)
            picked = slab_ref[j, rows, :]
            mid_ref[r, :, lanes] = picked.astype(_BF16)
            midslab_ref[j, r] = picked
        for r in range(far_dil):
            rows = pl.ds(r // MID_DIL, ROW_TILE // far_dil, stride=MID_DIL)
            far_ref[r, :, lanes] = midslab_ref[j, r % MID_DIL, rows, :].astype(_BF16)


def _inproj(h, w_in, pool_w_bd, pool_scale, cos, sin, layer):
    b, s, _ = h.shape
    tiles = s // ROW_TILE
    far_dil = MID_DIL * MID_DIL
    halo_per_tile = ROW_TILE // POOL_PAD
    last_halo = s // POOL_PAD - 1
    table = pl.BlockSpec((None, ROW_TILE, LANES), lambda i, t: (i, t, 0))
    return pl.pallas_call(
        _inproj_kernel,
        grid=(b, tiles),
        in_specs=[
            pl.BlockSpec((None, ROW_TILE, D_MODEL), lambda i, t: (i, t, 0)),
            pl.BlockSpec((None, POOL_PAD, D_MODEL),
                         lambda i, t: (i, jnp.maximum(t * halo_per_tile - 1, 0), 0)),
            pl.BlockSpec((None, POOL_PAD, D_MODEL),
                         lambda i, t: (i, jnp.minimum((t + 1) * halo_per_tile, last_halo), 0)),
            _resident((None, D_MODEL, IN_PROJ_WIDTH), lambda i, t: (layer, 0, 0)),
            pl.BlockSpec((None, POOL_WIDTH, POOL_WIDTH), lambda i, t: (layer, 0, 0)),
            pl.BlockSpec((None, 1, POOL_WIDTH), lambda i, t: (layer, 0, 0)),
            table, table,
        ],
        out_specs=[
            pl.BlockSpec((None, ROW_TILE, POOL_WIDTH), lambda i, t: (i, t, 0)),
            pl.BlockSpec((None, ROW_TILE, QKV_WIDTH), lambda i, t: (i, t, 0)),
            pl.BlockSpec((None, MID_DIL, ROW_TILE // MID_DIL, QKV_WIDTH),
                         lambda i, t: (i, 0, t, 0)),
            pl.BlockSpec((None, far_dil, ROW_TILE // far_dil, QKV_WIDTH),
                         lambda i, t: (i, 0, t, 0)),
        ],
        out_shape=[
            jax.ShapeDtypeStruct((b, s, POOL_WIDTH), _BF16),
            jax.ShapeDtypeStruct((b, s, QKV_WIDTH), _BF16),
            jax.ShapeDtypeStruct((b, MID_DIL, s // MID_DIL, QKV_WIDTH), _BF16),
            jax.ShapeDtypeStruct((b, far_dil, s // far_dil, QKV_WIDTH), _BF16),
        ],
        scratch_shapes=[
            pltpu.VMEM((QKV_WIDTH // LANES, ROW_TILE, LANES), _F32),
            pltpu.VMEM((QKV_WIDTH // LANES, MID_DIL, ROW_TILE // MID_DIL, LANES), _F32),
            pltpu.VMEM((ROW_TILE + 2 * POOL_PAD, POOL_WIDTH), _F32),
            pltpu.VMEM((tiles, ROW_TILE, POOL_WIDTH), _F32)],
        compiler_params=_params(2),
        name="inproj",
    )(h, h, h, w_in, pool_w_bd, pool_scale, cos, sin)


def _attn_kernel(qn_ref, kn_ref, vn_ref, qm_ref, km_ref, vm_ref, qf_ref, kf_ref, vf_ref, o_ref,
                 mask_ref, s_ref, p_ref, st_ref, num_far, m_far, l_far, num_mid, m_mid, l_mid):
    s_len = qn_ref.shape[1]
    far_dil = MID_DIL * MID_DIL
    operands = ((qn_ref, kn_ref, vn_ref), (qm_ref, km_ref, vm_ref), (qf_ref, kf_ref, vf_ref))
    first = lax.broadcasted_iota(jnp.int32, (Q_BLOCK, LANES), 1) < HEAD_DIM

    rr = lax.broadcasted_iota(jnp.int32, (Q_BLOCK, 2 * Q_BLOCK), 0)
    cc = lax.broadcasted_iota(jnp.int32, (Q_BLOCK, 2 * Q_BLOCK), 1)
    for idx in range(3):
        off = (idx - 2) * BAND_HALF
        mask_ref[idx] = jnp.where(jnp.abs(cc - rr + off) <= BAND_HALF, jnp.inf, MASK_VALUE)

    blocks = []
    for order, n_seq in ((2, far_dil), (1, MID_DIL), (0, 1)):
        seq = s_len // n_seq
        blocks += [(order, r, q0, seq) for r in range(n_seq) for q0 in range(0, seq, Q_BLOCK)]

    def key_window(q0, seq):
        kwin = min(2 * Q_BLOCK, seq)
        return min(max(q0 - BAND_HALF, 0), seq - kwin), kwin

    def scores(blk, slot):
        order, r, q0, seq = blk
        q_ref, k_ref, _ = operands[order]
        k0, kwin = key_window(q0, seq)
        q = q_ref[r, q0:q0 + Q_BLOCK, :]
        kblk = k_ref[r, k0:k0 + kwin, :]
        cap = mask_ref[(k0 - q0) // BAND_HALF + 2, :, 0:kwin]
        zero = jnp.zeros_like(q)
        for h, qh in enumerate((jnp.where(first, q, zero), jnp.where(first, zero, q))):
            sc = lax.dot_general(qh, kblk, (((1,), (1,)), ((), ())),
                                 preferred_element_type=_F32)
            s_ref[slot, h, :, 0:kwin] = jnp.minimum(sc, cap)

    def softmax(blk, slot):
        _, kwin = key_window(blk[2], blk[3])
        stats = []
        for h in range(2):
            sc = s_ref[slot, h, :, 0:kwin]
            m = jnp.max(sc, axis=-1, keepdims=True)
            p = jnp.exp2(sc - m)
            stats.append((m, jnp.sum(p, axis=-1, keepdims=True)))
            p_ref[slot, h, :, 0:kwin] = p.astype(_BF16)
        st_ref[slot, 0] = jnp.where(first, stats[0][0], stats[1][0])
        st_ref[slot, 1] = jnp.where(first, stats[0][1], stats[1][1])

    def values(blk, slot):
        order, r, q0, seq = blk
        v_ref = operands[order][2]
        k0, kwin = key_window(q0, seq)
        vblk = v_ref[r, k0:k0 + kwin, :]
        o = [jnp.dot(p_ref[slot, h, :, 0:kwin], vblk, preferred_element_type=_F32)
             for h in range(2)]
        num = jnp.where(first, o[0], o[1])
        m = st_ref[slot, 0]
        l = st_ref[slot, 1]
        rows = slice(q0, q0 + Q_BLOCK)
        if order == 2:
            out_rows = pl.ds(MID_DIL * q0 + r // MID_DIL, Q_BLOCK, stride=MID_DIL)
            num_far[r % MID_DIL, out_rows, :] = num
            m_far[r % MID_DIL, out_rows, :] = m
            l_far[r % MID_DIL, out_rows, :] = l
            return
        if order == 1:
            m_c, num_c, l_c = m_far[r, rows, :], num_far[r, rows, :], l_far[r, rows, :]
        else:
            m_c, num_c, l_c = m_mid[rows, :], num_mid[rows, :], l_mid[rows, :]
        m_new = jnp.maximum(m, m_c)
        a = jnp.exp2(m - m_new)
        a_c = jnp.exp2(m_c - m_new)
        num = a * num + a_c * num_c
        l = a * l + a_c * l_c
        if order == 1:
            out_rows = pl.ds(MID_DIL * q0 + r, Q_BLOCK, stride=MID_DIL)
            num_mid[out_rows, :] = num
            m_mid[out_rows, :] = m_new
            l_mid[out_rows, :] = l
        else:
            o_ref[rows, :] = (num / l).astype(o_ref.dtype)

    stages = (scores, softmax, values)
    for step in range(len(blocks) + len(stages) - 1):
        for depth, stage in enumerate(stages):
            i = step - depth
            if 0 <= i < len(blocks):
                stage(blocks[i], i % ATTN_SLOTS)


def _attention(nat, mid, far):
    b, s, _ = nat.shape
    n_pairs = ATTN_WIDTH // LANES
    far_dil = MID_DIL * MID_DIL

    def specs(n_seq):
        def tile(off):
            return pl.BlockSpec((None, n_seq, s // n_seq, LANES),
                                lambda i, j: (i, 0, 0, off + j))
        return [tile(0), tile(n_pairs), tile(2 * n_pairs)]

    return pl.pallas_call(
        _attn_kernel,
        grid=(b, n_pairs),
        in_specs=specs(1) + specs(MID_DIL) + specs(far_dil),
        out_specs=pl.BlockSpec((None, s, LANES), lambda i, j: (i, 0, j)),
        out_shape=jax.ShapeDtypeStruct((b, s, ATTN_WIDTH), _BF16),
        scratch_shapes=[pltpu.VMEM((3, Q_BLOCK, 2 * Q_BLOCK), _F32),
                        pltpu.VMEM((ATTN_SLOTS, 2, Q_BLOCK, 2 * Q_BLOCK), _F32),
                        pltpu.VMEM((ATTN_SLOTS, 2, Q_BLOCK, 2 * Q_BLOCK), _BF16),
                        pltpu.VMEM((ATTN_SLOTS, 2, Q_BLOCK, LANES), _F32)]
        + [pltpu.VMEM((MID_DIL, s // MID_DIL, LANES), _F32)] * 3
        + [pltpu.VMEM((s, LANES), _F32)] * 3,
        compiler_params=_params(2),
        name="dilated_attention",
    )(nat.reshape(b, 1, s, QKV_WIDTH), nat.reshape(b, 1, s, QKV_WIDTH),
      nat.reshape(b, 1, s, QKV_WIDTH), mid, mid, mid, far, far, far)


def _pool_block_diag(pool_w):
    g = POOL_GROUP_DIM
    z = jnp.zeros((DEPTH, POOL_WIDTH, POOL_WIDTH), pool_w.dtype)
    for k in range(POOL_WIDTH // g):
        z = z.at[:, k * g:(k + 1) * g, k * g:(k + 1) * g].set(pool_w[:, k])
    return z.astype(_BF16)


@jax.jit
def _forward(x, positions, ffn1_norm, ffn1_w_gate, ffn1_w_up, ffn1_w_down, mix_norm, w_in,
             pool_w, pool_scale, w_out, ffn2_norm, ffn2_w_gate, ffn2_w_up, ffn2_w_down,
             final_norm):
    b, s, d = x.shape
    n = b * s
    assert d == D_MODEL and s % ROW_TILE == 0 and s % (Q_BLOCK * MID_DIL * MID_DIL) == 0
    assert all(w // 2 == POOL_WINDOWS[0] * 2 ** i // 2 for i, w in enumerate(POOL_WINDOWS))

    pool_w_bd = _pool_block_diag(pool_w)
    g1 = ffn1_norm.reshape(DEPTH, 1, D_MODEL)
    g2 = ffn2_norm.reshape(DEPTH, 1, D_MODEL)
    gm = mix_norm.reshape(DEPTH, 1, D_MODEL)
    ps = pool_scale.reshape(DEPTH, 1, POOL_WIDTH)
    gf = final_norm.reshape(1, 1, D_MODEL)

    inv_freq = ROPE_THETA ** (-jnp.arange(0, ROPE_DIM, 2, dtype=_F32) / ROPE_DIM)
    invf = jnp.tile(inv_freq, LANES // (ROPE_DIM // 2)).reshape(1, LANES)
    pos = positions.astype(_F32)[..., None]
    cos, sin = _rope_tables(pos, invf)

    xs = x.reshape(n, D_MODEL)
    for layer in range(DEPTH):
        xs, h_mix = _ffn(xs, g1, ffn1_w_gate, ffn1_w_up, ffn1_w_down, layer,
                         tail="emit_norm", tail_gain=(gm, layer))
        y_pool, nat, mid, far = _inproj(h_mix.reshape(b, s, D_MODEL), w_in, pool_w_bd, ps, cos,
                                        sin, layer)
        y_attn = _attention(nat, mid, far)
        mixer = (y_pool.reshape(n, POOL_WIDTH), y_attn.reshape(n, ATTN_WIDTH), w_out)
        last = layer == DEPTH - 1
        xs, = _ffn(xs, g2, ffn2_w_gate, ffn2_w_up, ffn2_w_down, layer, mixer=mixer,
                   tail="final_norm" if last else None, tail_gain=(gf, 0) if last else None)
    return xs.reshape(b, s, D_MODEL)


def kernel(x, positions, ffn1_norm, ffn1_w_gate, ffn1_w_up, ffn1_w_down, mix_norm, w_in,
           pool_w, pool_scale, w_out, ffn2_norm, ffn2_w_gate, ffn2_w_up, ffn2_w_down,
           final_norm):
    return _forward(x, positions, ffn1_norm, ffn1_w_gate, ffn1_w_up, ffn1_w_down, mix_norm,
                    w_in, pool_w, pool_scale, w_out, ffn2_norm, ffn2_w_gate, ffn2_w_up,
                    ffn2_w_down, final_norm)
```

```python
import functools

import jax
import jax.numpy as jnp
from jax import lax
from jax.experimental import pallas as pl
from jax.experimental.pallas import tpu as pltpu

D_MODEL = 1024
DEPTH = 4
POOL_WIDTH = 256
POOL_WINDOWS = (2, 4, 8, 16)
POOL_GROUP_DIM = 64
HEAD_DIM = 64
ATTN_WIDTH = 768
DILATED_PATTERNS = ((128, 1), (512, 4), (2048, 16))
ROPE_THETA = 500000.0
ROPE_DIM = 16
D_FF = 2816
IN_PROJ_WIDTH = 2560
NORM_EPS = 1e-6
MASK_VALUE = -1e30
LOG2_E = 1.4426950408889634

LANES = 128
MXU_WIDTH = 256
ROW_TILE = 512
FF_CHUNK = MXU_WIDTH
N_FF_CHUNKS = D_FF // FF_CHUNK
Q_BLOCK = 128
BAND_HALF = 64
ATTN_PAIRS = 2
ATTN_SLOTS = 3
MID_DIL = 4
QKV_WIDTH = 3 * ATTN_WIDTH
POOL_PAD = 16
VMEM_LIMIT = 60 * 1024 * 1024

_F32 = jnp.float32
_BF16 = jnp.bfloat16

assert all(w // (2 * d) == BAND_HALF for w, d in DILATED_PATTERNS)
assert [d for _, d in DILATED_PATTERNS] == [1, MID_DIL, MID_DIL * MID_DIL]


def _params(n_grid_dims):
    return pltpu.CompilerParams(
        dimension_semantics=("arbitrary",) * n_grid_dims,
        vmem_limit_bytes=VMEM_LIMIT)


def _resident(block_shape, index_map):
    return pl.BlockSpec(block_shape, index_map, pipeline_mode=pl.Buffered(1))


def _rmsnorm(x, g):
    y = x * lax.rsqrt(jnp.mean(x * x, axis=-1, keepdims=True) + NORM_EPS)
    return y * g


def _ffn_kernel(*refs, mix_in, tail):
    refs = list(refs)
    x_ref = refs.pop(0)
    if mix_in:
        yp_ref, ya_ref, wo_ref = refs[:3]
        del refs[:3]
    g_ref, wg_ref, wu_ref, wd_ref = refs[:4]
    del refs[:4]
    tg_ref = refs.pop(0) if tail else None
    o_ref = refs.pop(0)
    hn_ref = refs.pop(0) if tail == "emit_norm" else None
    h_ref, acc_ref = refs

    x = x_ref[...]
    if mix_in:
        x = x + jnp.dot(yp_ref[...], wo_ref[0:POOL_WIDTH, :].astype(_BF16),
                        preferred_element_type=_F32)
        x = x + jnp.dot(ya_ref[...], wo_ref[POOL_WIDTH:D_MODEL, :].astype(_BF16),
                        preferred_element_type=_F32)
        o_ref[...] = x
    h_ref[...] = _rmsnorm(x, g_ref[...]).astype(_BF16)

    for c in range(N_FF_CHUNKS):
        cols = slice(c * FF_CHUNK, (c + 1) * FF_CHUNK)
        h = h_ref[...]
        gate = jnp.dot(h, wg_ref[:, cols].astype(_BF16), preferred_element_type=_F32)
        up = jnp.dot(h, wu_ref[:, cols].astype(_BF16), preferred_element_type=_F32)
        act = ((gate * jax.nn.sigmoid(gate)) * up).astype(_BF16)
        down = jnp.dot(act, wd_ref[cols, :].astype(_BF16), preferred_element_type=_F32)
        if c == 0:
            acc_ref[...] = down
        elif c < N_FF_CHUNKS - 1:
            acc_ref[...] += down
    base = o_ref[...] if mix_in else x_ref[...]
    y = base + 0.5 * (acc_ref[...] + down)
    if tail == "final_norm":
        y = _rmsnorm(y, tg_ref[...])
    o_ref[...] = y
    if tail == "emit_norm":
        hn_ref[...] = _rmsnorm(y, tg_ref[...]).astype(_BF16)


def _ffn(x, gains, wg, wu, wd, layer, mixer=None, tail=None, tail_gain=None):
    n = x.shape[0]
    rows = lambda width: pl.BlockSpec((ROW_TILE, width), lambda i: (i, 0))
    operands = [x]
    in_specs = [rows(D_MODEL)]
    if mixer is not None:
        operands += list(mixer)
        in_specs += [rows(POOL_WIDTH), rows(ATTN_WIDTH),
                     _resident((None, D_MODEL, D_MODEL), lambda i: (layer, 0, 0))]
    operands += [gains, wg, wu, wd]
    in_specs += [
        pl.BlockSpec((None, 1, D_MODEL), lambda i: (layer, 0, 0)),
        _resident((None, D_MODEL, D_FF), lambda i: (layer, 0, 0)),
        _resident((None, D_MODEL, D_FF), lambda i: (layer, 0, 0)),
        _resident((None, D_FF, D_MODEL), lambda i: (layer, 0, 0)),
    ]
    out_specs = [rows(D_MODEL)]
    out_shape = [jax.ShapeDtypeStruct((n, D_MODEL), _F32)]
    if tail is not None:
        gain_stack, gain_index = tail_gain
        operands.append(gain_stack)
        in_specs.append(pl.BlockSpec((None, 1, D_MODEL), lambda i: (gain_index, 0, 0)))
    if tail == "emit_norm":
        out_specs.append(rows(D_MODEL))
        out_shape.append(jax.ShapeDtypeStruct((n, D_MODEL), _BF16))
    return pl.pallas_call(
        functools.partial(_ffn_kernel, mix_in=mixer is not None, tail=tail),
        grid=(n // ROW_TILE,),
        in_specs=in_specs,
        out_specs=out_specs,
        out_shape=out_shape,
        scratch_shapes=[pltpu.VMEM((ROW_TILE, D_MODEL), _BF16),
                        pltpu.VMEM((ROW_TILE, D_MODEL), _F32)],
        compiler_params=_params(1),
        name="ffn",
    )(*operands)


def _rope_table_kernel(pos_ref, invf_ref, cos_ref, sin_ref):
    ang = pos_ref[...] * invf_ref[...]
    c = jnp.cos(ang)
    s = jnp.sin(ang)
    d = lax.broadcasted_iota(jnp.int32, ang.shape, 1) & (HEAD_DIM - 1)
    cos_ref[...] = jnp.where(d < ROPE_DIM, c, 1.0)
    sin_ref[...] = jnp.where(d < ROPE_DIM // 2, -s, jnp.where(d < ROPE_DIM, s, 0.0))


def _rope_tables(pos, invf):
    b, s, _ = pos.shape
    out = jax.ShapeDtypeStruct((b, s, LANES), _F32)
    spec = pl.BlockSpec((None, s, LANES), lambda i: (i, 0, 0))
    return pl.pallas_call(
        _rope_table_kernel,
        grid=(b,),
        in_specs=[pl.BlockSpec((None, s, 1), lambda i: (i, 0, 0)),
                  pl.BlockSpec((1, LANES), lambda i: (0, 0))],
        out_specs=[spec, spec],
        out_shape=[out, out],
        compiler_params=_params(1),
        name="rope_tables",
    )(pos, invf)


def _pool_head_group(h_ref, hprev_ref, hnext_ref, w_ref, pw_ref, ps_ref, ypool_ref,
                     pad_ref, cnt_ref):
    tile = pl.program_id(1)
    last_tile = pl.num_programs(1) - 1
    g = POOL_GROUP_DIM

    @pl.when(pl.program_id(0) == 0)
    def _():
        col = lax.broadcasted_iota(jnp.int32, (ROW_TILE, POOL_WIDTH), 1)
        half_w = jnp.where(col < g, POOL_WINDOWS[0] // 2,
                           jnp.where(col < 2 * g, POOL_WINDOWS[1] // 2,
                                     jnp.where(col < 3 * g, POOL_WINDOWS[2] // 2,
                                               POOL_WINDOWS[3] // 2)))
        pos = lax.broadcasted_iota(jnp.int32, (ROW_TILE, POOL_WIDTH), 0) + tile * ROW_TILE
        seq_len = pl.num_programs(1) * ROW_TILE
        lo = jnp.maximum(pos - half_w, 0)
        hi = jnp.minimum(pos + half_w - 1, seq_len - 1)
        cnt_ref[tile] = (hi - lo + 1).astype(_F32)

    w_pool = w_ref[:, 0:POOL_WIDTH].astype(_BF16)
    v = jnp.dot(h_ref[...], w_pool, preferred_element_type=_F32)
    v_prev = jnp.dot(hprev_ref[...], w_pool, preferred_element_type=_F32)
    v_next = jnp.dot(hnext_ref[...], w_pool, preferred_element_type=_F32)
    pad_ref[0:POOL_PAD, :] = jnp.where(tile > 0, v_prev, 0.0)
    pad_ref[POOL_PAD:POOL_PAD + ROW_TILE, :] = v
    pad_ref[POOL_PAD + ROW_TILE:, :] = jnp.where(tile < last_tile, v_next, 0.0)
    x = pad_ref[...]
    rows = x.shape[0]

    def shifted(t, k):
        return pltpu.roll(t, k % rows, 0)

    w2 = x + shifted(x, 1)
    w4 = shifted(w2, 1) + shifted(w2, -1)
    w8 = shifted(w4, 2) + shifted(w4, -2)
    w16 = shifted(w8, 4) + shifted(w8, -4)
    lane = lax.broadcasted_iota(jnp.int32, x.shape, 1)
    wsum = jnp.where(lane < g, w2, jnp.where(lane < 2 * g, w4, jnp.where(lane < 3 * g, w8, w16)))
    wsum = wsum[POOL_PAD:POOL_PAD + ROW_TILE, :]
    diff = wsum / cnt_ref[tile] - v
    y = jnp.dot(diff.astype(_BF16), pw_ref[...], preferred_element_type=_F32)
    ypool_ref[...] = (y * ps_ref[...]).astype(ypool_ref.dtype)


def _inproj_kernel(h_ref, hprev_ref, hnext_ref, w_ref, pw_ref, ps_ref, cos_ref, sin_ref,
                   ypool_ref, nat_ref, mid_ref, far_ref,
                   slab_ref, midslab_ref, pad_ref, cnt_ref):
    _pool_head_group(h_ref, hprev_ref, hnext_ref, w_ref, pw_ref, ps_ref, ypool_ref,
                     pad_ref, cnt_ref)
    h = h_ref[...]
    cos = cos_ref[...]
    sin = sin_ref[...]
    half_rope = ROPE_DIM // 2
    n_heads_tiles = ATTN_WIDTH // LANES
    on_t1 = (lax.broadcasted_iota(jnp.int32, cos.shape, 1) & (HEAD_DIM - 1)) < half_rope

    def rope(t):
        partner = jnp.where(on_t1, pltpu.roll(t, LANES - half_rope, 1), pltpu.roll(t, half_rope, 1))
        return t * cos + partner * sin

    for c in range(POOL_WIDTH // MXU_WIDTH, IN_PROJ_WIDTH // MXU_WIDTH):
        cols = slice(c * MXU_WIDTH, (c + 1) * MXU_WIDTH)
        part = jnp.dot(h, w_ref[:, cols].astype(_BF16), preferred_element_type=_F32)
        for half in range(MXU_WIDTH // LANES):
            j = (c * MXU_WIDTH - POOL_WIDTH) // LANES + half
            t = part[:, half * LANES:(half + 1) * LANES]
            if j < n_heads_tiles:
                t = rope(t) * (LOG2_E / (HEAD_DIM ** 0.5))
            elif j < 2 * n_heads_tiles:
                t = rope(t)
            slab_ref[j] = t
            nat_ref[:, j * LANES:(j + 1) * LANES] = t.astype(_BF16)

    far_dil = MID_DIL * MID_DIL
    for j in range(QKV_WIDTH // LANES):
        lanes = slice(j * LANES, (j + 1) * LANES)
        for r in range(MID_DIL):
            rows = pl.ds(r, ROW_TILE // MID_DIL, stride=MID_DIL)
            picked = slab_ref[j, rows, :]
            mid_ref[r, :, lanes] = picked.astype(_BF16)
            midslab_ref[j, r] = picked
        for r in range(far_dil):
            rows = pl.ds(r // MID_DIL, ROW_TILE // far_dil, stride=MID_DIL)
            far_ref[r, :, lanes] = midslab_ref[j, r % MID_DIL, rows, :].astype(_BF16)


def _inproj(h, w_in, pool_w_bd, pool_scale, cos, sin, layer):
    b, s, _ = h.shape
    tiles = s // ROW_TILE
    far_dil = MID_DIL * MID_DIL
    halo_per_tile = ROW_TILE // POOL_PAD
    last_halo = s // POOL_PAD - 1
    table = pl.BlockSpec((None, ROW_TILE, LANES), lambda i, t: (i, t, 0))
    return pl.pallas_call(
        _inproj_kernel,
        grid=(b, tiles),
        in_specs=[
            pl.BlockSpec((None, ROW_TILE, D_MODEL), lambda i, t: (i, t, 0)),
            pl.BlockSpec((None, POOL_PAD, D_MODEL),
                         lambda i, t: (i, jnp.maximum(t * halo_per_tile - 1, 0), 0)),
            pl.BlockSpec((None, POOL_PAD, D_MODEL),
                         lambda i, t: (i, jnp.minimum((t + 1) * halo_per_tile, last_halo), 0)),
            _resident((None, D_MODEL, IN_PROJ_WIDTH), lambda i, t: (layer, 0, 0)),
            pl.BlockSpec((None, POOL_WIDTH, POOL_WIDTH), lambda i, t: (layer, 0, 0)),
            pl.BlockSpec((None, 1, POOL_WIDTH), lambda i, t: (layer, 0, 0)),
            table, table,
        ],
        out_specs=[
            pl.BlockSpec((None, ROW_TILE, POOL_WIDTH), lambda i, t: (i, t, 0)),
            pl.BlockSpec((None, ROW_TILE, QKV_WIDTH), lambda i, t: (i, t, 0)),
            pl.BlockSpec((None, MID_DIL, ROW_TILE // MID_DIL, QKV_WIDTH),
                         lambda i, t: (i, 0, t, 0)),
            pl.BlockSpec((None, far_dil, ROW_TILE // far_dil, QKV_WIDTH),
                         lambda i, t: (i, 0, t, 0)),
        ],
        out_shape=[
            jax.ShapeDtypeStruct((b, s, POOL_WIDTH), _BF16),
            jax.ShapeDtypeStruct((b, s, QKV_WIDTH), _BF16),
            jax.ShapeDtypeStruct((b, MID_DIL, s // MID_DIL, QKV_WIDTH), _BF16),
            jax.ShapeDtypeStruct((b, far_dil, s // far_dil, QKV_WIDTH), _BF16),
        ],
        scratch_shapes=[
            pltpu.VMEM((QKV_WIDTH // LANES, ROW_TILE, LANES), _F32),
            pltpu.VMEM((QKV_WIDTH // LANES, MID_DIL, ROW_TILE // MID_DIL, LANES), _F32),
            pltpu.VMEM((ROW_TILE + 2 * POOL_PAD, POOL_WIDTH), _F32),
            pltpu.VMEM((tiles, ROW_TILE, POOL_WIDTH), _F32)],
        compiler_params=_params(2),
        name="inproj",
    )(h, h, h, w_in, pool_w_bd, pool_scale, cos, sin)


def _attn_kernel(qn_ref, kn_ref, vn_ref, qm_ref, km_ref, vm_ref, qf_ref, kf_ref, vf_ref, o_ref,
                 mask_ref, s_ref, p_ref, st_ref, num_far, m_far, l_far, num_mid, m_mid, l_mid):
    s_len = qn_ref.shape[1]
    far_dil = MID_DIL * MID_DIL
    operands = ((qn_ref, kn_ref, vn_ref), (qm_ref, km_ref, vm_ref), (qf_ref, kf_ref, vf_ref))
    first = lax.broadcasted_iota(jnp.int32, (Q_BLOCK, LANES), 1) < HEAD_DIM

    rr = lax.broadcasted_iota(jnp.int32, (Q_BLOCK, 2 * Q_BLOCK), 0)
    cc = lax.broadcasted_iota(jnp.int32, (Q_BLOCK, 2 * Q_BLOCK), 1)
    for idx in range(3):
        off = (idx - 2) * BAND_HALF
        mask_ref[idx] = jnp.where(jnp.abs(cc - rr + off) <= BAND_HALF, jnp.inf, MASK_VALUE)

    blocks = []
    for order, n_seq in ((2, far_dil), (1, MID_DIL), (0, 1)):
        seq = s_len // n_seq
        blocks += [(pair, order, r, q0, seq) for r in range(n_seq)
                   for q0 in range(0, seq, Q_BLOCK) for pair in range(ATTN_PAIRS)]

    def key_window(q0, seq):
        kwin = min(2 * Q_BLOCK, seq)
        return min(max(q0 - BAND_HALF, 0), seq - kwin), kwin

    def pair_lanes(pair):
        return slice(pair * LANES, (pair + 1) * LANES)

    def scores(blk, slot):
        pair, order, r, q0, seq = blk
        q_ref, k_ref, _ = operands[order]
        k0, kwin = key_window(q0, seq)
        q = q_ref[r, q0:q0 + Q_BLOCK, pair_lanes(pair)]
        kblk = k_ref[r, k0:k0 + kwin, pair_lanes(pair)]
        cap = mask_ref[(k0 - q0) // BAND_HALF + 2, :, 0:kwin]
        zero = jnp.zeros_like(q)
        for h, qh in enumerate((jnp.where(first, q, zero), jnp.where(first, zero, q))):
            sc = lax.dot_general(qh, kblk, (((1,), (1,)), ((), ())),
                                 preferred_element_type=_F32)
            s_ref[slot, h, :, 0:kwin] = jnp.minimum(sc, cap)

    def softmax(blk, slot):
        _, kwin = key_window(blk[3], blk[4])
        stats = []
        for h in range(2):
            sc = s_ref[slot, h, :, 0:kwin]
            m = jnp.max(sc, axis=-1, keepdims=True)
            p = jnp.exp2(sc - m)
            stats.append((m, jnp.sum(p, axis=-1, keepdims=True)))
            p_ref[slot, h, :, 0:kwin] = p.astype(_BF16)
        st_ref[slot, 0] = jnp.where(first, stats[0][0], stats[1][0])
        st_ref[slot, 1] = jnp.where(first, stats[0][1], stats[1][1])

    def values(blk, slot):
        pair, order, r, q0, seq = blk
        v_ref = operands[order][2]
        k0, kwin = key_window(q0, seq)
        vblk = v_ref[r, k0:k0 + kwin, pair_lanes(pair)]
        o = [jnp.dot(p_ref[slot, h, :, 0:kwin], vblk, preferred_element_type=_F32)
             for h in range(2)]
        num = jnp.where(first, o[0], o[1])
        m = st_ref[slot, 0]
        l = st_ref[slot, 1]
        rows = slice(q0, q0 + Q_BLOCK)
        if order == 2:
            out_rows = pl.ds(MID_DIL * q0 + r // MID_DIL, Q_BLOCK, stride=MID_DIL)
            num_far[pair, r % MID_DIL, out_rows, :] = num
            m_far[pair, r % MID_DIL, out_rows, :] = m
            l_far[pair, r % MID_DIL, out_rows, :] = l
            return
        if order == 1:
            m_c, num_c, l_c = (ref[pair, r, rows, :] for ref in (m_far, num_far, l_far))
        else:
            m_c, num_c, l_c = (ref[pair, rows, :] for ref in (m_mid, num_mid, l_mid))
        m_new = jnp.maximum(m, m_c)
        a = jnp.exp2(m - m_new)
        a_c = jnp.exp2(m_c - m_new)
        num = a * num + a_c * num_c
        l = a * l + a_c * l_c
        if order == 1:
            out_rows = pl.ds(MID_DIL * q0 + r, Q_BLOCK, stride=MID_DIL)
            num_mid[pair, out_rows, :] = num
            m_mid[pair, out_rows, :] = m_new
            l_mid[pair, out_rows, :] = l
        else:
            o_ref[rows, pair_lanes(pair)] = (num / l).astype(o_ref.dtype)

    stages = (scores, softmax, values)
    for step in range(len(blocks) + len(stages) - 1):
        for depth, stage in enumerate(stages):
            i = step - depth
            if 0 <= i < len(blocks):
                stage(blocks[i], i % ATTN_SLOTS)


def _attention(nat, mid, far):
    b, s, _ = nat.shape
    n_pairs = ATTN_WIDTH // LANES
    far_dil = MID_DIL * MID_DIL

    steps = n_pairs // ATTN_PAIRS
    width = ATTN_PAIRS * LANES

    def specs(n_seq):
        def tile(off):
            return pl.BlockSpec((None, n_seq, s // n_seq, width),
                                lambda i, j: (i, 0, 0, off + j))
        return [tile(0), tile(steps), tile(2 * steps)]

    return pl.pallas_call(
        _attn_kernel,
        grid=(b, steps),
        in_specs=specs(1) + specs(MID_DIL) + specs(far_dil),
        out_specs=pl.BlockSpec((None, s, width), lambda i, j: (i, 0, j)),
        out_shape=jax.ShapeDtypeStruct((b, s, ATTN_WIDTH), _BF16),
        scratch_shapes=[pltpu.VMEM((3, Q_BLOCK, 2 * Q_BLOCK), _F32),
                        pltpu.VMEM((ATTN_SLOTS, 2, Q_BLOCK, 2 * Q_BLOCK), _F32),
                        pltpu.VMEM((ATTN_SLOTS, 2, Q_BLOCK, 2 * Q_BLOCK), _BF16),
                        pltpu.VMEM((ATTN_SLOTS, 2, Q_BLOCK, LANES), _F32)]
        + [pltpu.VMEM((ATTN_PAIRS, MID_DIL, s // MID_DIL, LANES), _F32)] * 3
        + [pltpu.VMEM((ATTN_PAIRS, s, LANES), _F32)] * 3,
        compiler_params=_params(2),
        name="dilated_attention",
    )(nat.reshape(b, 1, s, QKV_WIDTH), nat.reshape(b, 1, s, QKV_WIDTH),
      nat.reshape(b, 1, s, QKV_WIDTH), mid, mid, mid, far, far, far)


def _pool_block_diag(pool_w):
    g = POOL_GROUP_DIM
    z = jnp.zeros((DEPTH, POOL_WIDTH, POOL_WIDTH), pool_w.dtype)
    for k in range(POOL_WIDTH // g):
        z = z.at[:, k * g:(k + 1) * g, k * g:(k + 1) * g].set(pool_w[:, k])
    return z.astype(_BF16)


@jax.jit
def _forward(x, positions, ffn1_norm, ffn1_w_gate, ffn1_w_up, ffn1_w_down, mix_norm, w_in,
             pool_w, pool_scale, w_out, ffn2_norm, ffn2_w_gate, ffn2_w_up, ffn2_w_down,
             final_norm):
    b, s, d = x.shape
    n = b * s
    assert d == D_MODEL and s % ROW_TILE == 0 and s % (Q_BLOCK * MID_DIL * MID_DIL) == 0
    assert all(w // 2 == POOL_WINDOWS[0] * 2 ** i // 2 for i, w in enumerate(POOL_WINDOWS))

    pool_w_bd = _pool_block_diag(pool_w)
    g1 = ffn1_norm.reshape(DEPTH, 1, D_MODEL)
    g2 = ffn2_norm.reshape(DEPTH, 1, D_MODEL)
    gm = mix_norm.reshape(DEPTH, 1, D_MODEL)
    ps = pool_scale.reshape(DEPTH, 1, POOL_WIDTH)
    gf = final_norm.reshape(1, 1, D_MODEL)

    inv_freq = ROPE_THETA ** (-jnp.arange(0, ROPE_DIM, 2, dtype=_F32) / ROPE_DIM)
    invf = jnp.tile(inv_freq, LANES // (ROPE_DIM // 2)).reshape(1, LANES)
    pos = positions.astype(_F32)[..., None]
    cos, sin = _rope_tables(pos, invf)

    xs = x.reshape(n, D_MODEL)
    for layer in range(DEPTH):
        xs, h_mix = _ffn(xs, g1, ffn1_w_gate, ffn1_w_up, ffn1_w_down, layer,
                         tail="emit_norm", tail_gain=(gm, layer))
        y_pool, nat, mid, far = _inproj(h_mix.reshape(b, s, D_MODEL), w_in, pool_w_bd, ps, cos,
                                        sin, layer)
        y_attn = _attention(nat, mid, far)
        mixer = (y_pool.reshape(n, POOL_WIDTH), y_attn.reshape(n, ATTN_WIDTH), w_out)
        last = layer == DEPTH - 1
        xs, = _ffn(xs, g2, ffn2_w_gate, ffn2_w_up, ffn2_w_down, layer, mixer=mixer,
                   tail="final_norm" if last else None, tail_gain=(gf, 0) if last else None)
    return xs.reshape(b, s, D_MODEL)


def kernel(x, positions, ffn1_norm, ffn1_w_gate, ffn1_w_up, ffn1_w_down, mix_norm, w_in,
           pool_w, pool_scale, w_out, ffn2_norm, ffn2_w_gate, ffn2_w_up, ffn2_w_down,
           final_norm):
    return _forward(x, positions, ffn1_norm, ffn1_w_gate, ffn1_w_up, ffn1_w_down, mix_norm,
                    w_in, pool_w, pool_scale, w_out, ffn2_norm, ffn2_w_gate, ffn2_w_up,
                    ffn2_w_down, final_norm)
```

```python
import functools

import jax
import jax.numpy as jnp
from jax import lax
from jax.experimental import pallas as pl
from jax.experimental.pallas import tpu as pltpu

D_MODEL = 1024
DEPTH = 4
POOL_WIDTH = 256
POOL_WINDOWS = (2, 4, 8, 16)
POOL_GROUP_DIM = 64
HEAD_DIM = 64
ATTN_WIDTH = 768
DILATED_PATTERNS = ((128, 1), (512, 4), (2048, 16))
ROPE_THETA = 500000.0
ROPE_DIM = 16
D_FF = 2816
IN_PROJ_WIDTH = 2560
NORM_EPS = 1e-6
MASK_VALUE = -1e30
LOG2_E = 1.4426950408889634

LANES = 128
MXU_WIDTH = 256
ROW_TILE = 512
FF_CHUNK = MXU_WIDTH
N_FF_CHUNKS = D_FF // FF_CHUNK
Q_BLOCK = 128
BAND_HALF = 64
ATTN_PAIRS = 2
ATTN_SLOTS = 3
MID_DIL = 4
QKV_WIDTH = 3 * ATTN_WIDTH
POOL_PAD = 16
VMEM_LIMIT = 60 * 1024 * 1024

_F32 = jnp.float32
_BF16 = jnp.bfloat16

assert all(w // (2 * d) == BAND_HALF for w, d in DILATED_PATTERNS)
assert [d for _, d in DILATED_PATTERNS] == [1, MID_DIL, MID_DIL * MID_DIL]


def _params(n_grid_dims):
    return pltpu.CompilerParams(
        dimension_semantics=("arbitrary",) * n_grid_dims,
        vmem_limit_bytes=VMEM_LIMIT)


def _resident(block_shape, index_map):
    return pl.BlockSpec(block_shape, index_map, pipeline_mode=pl.Buffered(1))


def _rmsnorm(x, g):
    y = x * lax.rsqrt(jnp.mean(x * x, axis=-1, keepdims=True) + NORM_EPS)
    return y * g


def _ffn_kernel(*refs, layer, mix_in, tail):
    refs = list(refs)
    x_ref = refs.pop(0)
    if mix_in:
        yp_ref, ya_ref, wo_ref = refs[:3]
        del refs[:3]
    g_ref, wg_hbm, wu_hbm, wd_hbm = refs[:4]
    del refs[:4]
    tg_ref = refs.pop(0) if tail else None
    o_ref = refs.pop(0)
    hn_ref = refs.pop(0) if tail == "emit_norm" else None
    h_ref, acc_ref, wg_ref, wu_ref, wd_ref, sem = refs

    def weight_copy(kind, c):
        cols = pl.ds(c * FF_CHUNK, FF_CHUNK)
        src, dst = ((wg_hbm.at[layer, :, cols], wg_ref.at[c]),
                    (wu_hbm.at[layer, :, cols], wu_ref.at[c]),
                    (wd_hbm.at[layer, cols, :], wd_ref.at[c]))[kind]
        return pltpu.make_async_copy(src, dst, sem.at[kind, c])

    mixer_refs = (yp_ref, ya_ref, wo_ref) if mix_in else None
    step = functools.partial(_ffn_step, x_ref, mixer_refs, g_ref, wg_ref, wu_ref, wd_ref, tg_ref,
                             o_ref, hn_ref, h_ref, acc_ref, tail=tail)
    first_step = pl.program_id(0) == 0

    @pl.when(first_step)
    def _():
        for c in range(N_FF_CHUNKS):
            for kind in range(3):
                weight_copy(kind, c).start()
        step(weight_copy)

    @pl.when(jnp.logical_not(first_step))
    def _():
        step(None)


def _ffn_step(x_ref, mixer_refs, g_ref, wg_ref, wu_ref, wd_ref, tg_ref, o_ref, hn_ref, h_ref,
              acc_ref, weight_copy, *, tail):
    mix_in = mixer_refs is not None
    if mix_in:
        yp_ref, ya_ref, wo_ref = mixer_refs
    x = x_ref[...]
    if mix_in:
        x = x + jnp.dot(yp_ref[...], wo_ref[0:POOL_WIDTH, :].astype(_BF16),
                        preferred_element_type=_F32)
        x = x + jnp.dot(ya_ref[...], wo_ref[POOL_WIDTH:D_MODEL, :].astype(_BF16),
                        preferred_element_type=_F32)
        o_ref[...] = x
    h_ref[...] = _rmsnorm(x, g_ref[...]).astype(_BF16)

    for c in range(N_FF_CHUNKS):
        if weight_copy is not None:
            for kind in range(3):
                weight_copy(kind, c).wait()
        h = h_ref[...]
        gate = jnp.dot(h, wg_ref[c].astype(_BF16), preferred_element_type=_F32)
        up = jnp.dot(h, wu_ref[c].astype(_BF16), preferred_element_type=_F32)
        act = ((gate * jax.nn.sigmoid(gate)) * up).astype(_BF16)
        down = jnp.dot(act, wd_ref[c].astype(_BF16), preferred_element_type=_F32)
        if c == 0:
            acc_ref[...] = down
        elif c < N_FF_CHUNKS - 1:
            acc_ref[...] += down
    base = o_ref[...] if mix_in else x_ref[...]
    y = base + 0.5 * (acc_ref[...] + down)
    if tail == "final_norm":
        y = _rmsnorm(y, tg_ref[...])
    o_ref[...] = y
    if tail == "emit_norm":
        hn_ref[...] = _rmsnorm(y, tg_ref[...]).astype(_BF16)


def _ffn(x, gains, wg, wu, wd, layer, mixer=None, tail=None, tail_gain=None):
    n = x.shape[0]
    rows = lambda width: pl.BlockSpec((ROW_TILE, width), lambda i: (i, 0))
    operands = [x]
    in_specs = [rows(D_MODEL)]
    if mixer is not None:
        operands += list(mixer)
        in_specs += [rows(POOL_WIDTH), rows(ATTN_WIDTH),
                     _resident((None, D_MODEL, D_MODEL), lambda i: (layer, 0, 0))]
    operands += [gains, wg, wu, wd]
    in_specs += [pl.BlockSpec((None, 1, D_MODEL), lambda i: (layer, 0, 0))]
    in_specs += [pl.BlockSpec(memory_space=pl.ANY)] * 3
    out_specs = [rows(D_MODEL)]
    out_shape = [jax.ShapeDtypeStruct((n, D_MODEL), _F32)]
    if tail is not None:
        gain_stack, gain_index = tail_gain
        operands.append(gain_stack)
        in_specs.append(pl.BlockSpec((None, 1, D_MODEL), lambda i: (gain_index, 0, 0)))
    if tail == "emit_norm":
        out_specs.append(rows(D_MODEL))
        out_shape.append(jax.ShapeDtypeStruct((n, D_MODEL), _BF16))
    return pl.pallas_call(
        functools.partial(_ffn_kernel, layer=layer, mix_in=mixer is not None, tail=tail),
        grid=(n // ROW_TILE,),
        in_specs=in_specs,
        out_specs=out_specs,
        out_shape=out_shape,
        scratch_shapes=[pltpu.VMEM((ROW_TILE, D_MODEL), _BF16),
                        pltpu.VMEM((ROW_TILE, D_MODEL), _F32),
                        pltpu.VMEM((N_FF_CHUNKS, D_MODEL, FF_CHUNK), _F32),
                        pltpu.VMEM((N_FF_CHUNKS, D_MODEL, FF_CHUNK), _F32),
                        pltpu.VMEM((N_FF_CHUNKS, FF_CHUNK, D_MODEL), _F32),
                        pltpu.SemaphoreType.DMA((3, N_FF_CHUNKS))],
        compiler_params=_params(1),
        name="ffn",
    )(*operands)


def _rope_table_kernel(pos_ref, invf_ref, cos_ref, sin_ref):
    ang = pos_ref[...] * invf_ref[...]
    c = jnp.cos(ang)
    s = jnp.sin(ang)
    d = lax.broadcasted_iota(jnp.int32, ang.shape, 1) & (HEAD_DIM - 1)
    cos_ref[...] = jnp.where(d < ROPE_DIM, c, 1.0)
    sin_ref[...] = jnp.where(d < ROPE_DIM // 2, -s, jnp.where(d < ROPE_DIM, s, 0.0))


def _rope_tables(pos, invf):
    b, s, _ = pos.shape
    out = jax.ShapeDtypeStruct((b, s, LANES), _F32)
    spec = pl.BlockSpec((None, s, LANES), lambda i: (i, 0, 0))
    return pl.pallas_call(
        _rope_table_kernel,
        grid=(b,),
        in_specs=[pl.BlockSpec((None, s, 1), lambda i: (i, 0, 0)),
                  pl.BlockSpec((1, LANES), lambda i: (0, 0))],
        out_specs=[spec, spec],
        out_shape=[out, out],
        compiler_params=_params(1),
        name="rope_tables",
    )(pos, invf)


def _pool_head_group(h_ref, hprev_ref, hnext_ref, w_ref, pw_ref, ps_ref, ypool_ref,
                     pad_ref, cnt_ref):
    tile = pl.program_id(1)
    last_tile = pl.num_programs(1) - 1
    g = POOL_GROUP_DIM

    @pl.when(pl.program_id(0) == 0)
    def _():
        col = lax.broadcasted_iota(jnp.int32, (ROW_TILE, POOL_WIDTH), 1)
        half_w = jnp.where(col < g, POOL_WINDOWS[0] // 2,
                           jnp.where(col < 2 * g, POOL_WINDOWS[1] // 2,
                                     jnp.where(col < 3 * g, POOL_WINDOWS[2] // 2,
                                               POOL_WINDOWS[3] // 2)))
        pos = lax.broadcasted_iota(jnp.int32, (ROW_TILE, POOL_WIDTH), 0) + tile * ROW_TILE
        seq_len = pl.num_programs(1) * ROW_TILE
        lo = jnp.maximum(pos - half_w, 0)
        hi = jnp.minimum(pos + half_w - 1, seq_len - 1)
        cnt_ref[tile] = (hi - lo + 1).astype(_F32)

    w_pool = w_ref[:, 0:POOL_WIDTH].astype(_BF16)
    v = jnp.dot(h_ref[...], w_pool, preferred_element_type=_F32)
    v_prev = jnp.dot(hprev_ref[...], w_pool, preferred_element_type=_F32)
    v_next = jnp.dot(hnext_ref[...], w_pool, preferred_element_type=_F32)
    pad_ref[0:POOL_PAD, :] = jnp.where(tile > 0, v_prev, 0.0)
    pad_ref[POOL_PAD:POOL_PAD + ROW_TILE, :] = v
    pad_ref[POOL_PAD + ROW_TILE:, :] = jnp.where(tile < last_tile, v_next, 0.0)
    x = pad_ref[...]
    rows = x.shape[0]

    def shifted(t, k):
        return pltpu.roll(t, k % rows, 0)

    w2 = x + shifted(x, 1)
    w4 = shifted(w2, 1) + shifted(w2, -1)
    w8 = shifted(w4, 2) + shifted(w4, -2)
    w16 = shifted(w8, 4) + shifted(w8, -4)
    lane = lax.broadcasted_iota(jnp.int32, x.shape, 1)
    wsum = jnp.where(lane < g, w2, jnp.where(lane < 2 * g, w4, jnp.where(lane < 3 * g, w8, w16)))
    wsum = wsum[POOL_PAD:POOL_PAD + ROW_TILE, :]
    diff = wsum / cnt_ref[tile] - v
    y = jnp.dot(diff.astype(_BF16), pw_ref[...], preferred_element_type=_F32)
    ypool_ref[...] = (y * ps_ref[...]).astype(ypool_ref.dtype)


def _inproj_kernel(h_ref, hprev_ref, hnext_ref, w_ref, pw_ref, ps_ref, cos_ref, sin_ref,
                   ypool_ref, nat_ref, mid_ref, far_ref,
                   slab_ref, midslab_ref, pad_ref, cnt_ref):
    _pool_head_group(h_ref, hprev_ref, hnext_ref, w_ref, pw_ref, ps_ref, ypool_ref,
                     pad_ref, cnt_ref)
    h = h_ref[...]
    cos = cos_ref[...]
    sin = sin_ref[...]
    half_rope = ROPE_DIM // 2
    n_heads_tiles = ATTN_WIDTH // LANES
    on_t1 = (lax.broadcasted_iota(jnp.int32, cos.shape, 1) & (HEAD_DIM - 1)) < half_rope

    def rope(t):
        partner = jnp.where(on_t1, pltpu.roll(t, LANES - half_rope, 1), pltpu.roll(t, half_rope, 1))
        return t * cos + partner * sin

    for c in range(POOL_WIDTH // MXU_WIDTH, IN_PROJ_WIDTH // MXU_WIDTH):
        cols = slice(c * MXU_WIDTH, (c + 1) * MXU_WIDTH)
        part = jnp.dot(h, w_ref[:, cols].astype(_BF16), preferred_element_type=_F32)
        for half in range(MXU_WIDTH // LANES):
            j = (c * MXU_WIDTH - POOL_WIDTH) // LANES + half
            t = part[:, half * LANES:(half + 1) * LANES]
            if j < n_heads_tiles:
                t = rope(t) * (LOG2_E / (HEAD_DIM ** 0.5))
            elif j < 2 * n_heads_tiles:
                t = rope(t)
            slab_ref[j] = t
            nat_ref[:, j * LANES:(j + 1) * LANES] = t.astype(_BF16)

    far_dil = MID_DIL * MID_DIL
    for j in range(QKV_WIDTH // LANES):
        lanes = slice(j * LANES, (j + 1) * LANES)
        for r in range(MID_DIL):
            rows = pl.ds(r, ROW_TILE // MID_DIL, stride=MID_DIL)
            picked = slab_ref[j, rows, :]
            mid_ref[r, :, lanes] = picked.astype(_BF16)
            midslab_ref[j, r] = picked
        for r in range(far_dil):
            rows = pl.ds(r // MID_DIL, ROW_TILE // far_dil, stride=MID_DIL)
            far_ref[r, :, lanes] = midslab_ref[j, r % MID_DIL, rows, :].astype(_BF16)


def _inproj(h, w_in, pool_w_bd, pool_scale, cos, sin, layer):
    b, s, _ = h.shape
    tiles = s // ROW_TILE
    far_dil = MID_DIL * MID_DIL
    halo_per_tile = ROW_TILE // POOL_PAD
    last_halo = s // POOL_PAD - 1
    table = pl.BlockSpec((None, ROW_TILE, LANES), lambda i, t: (i, t, 0))
    return pl.pallas_call(
        _inproj_kernel,
        grid=(b, tiles),
        in_specs=[
            pl.BlockSpec((None, ROW_TILE, D_MODEL), lambda i, t: (i, t, 0)),
            pl.BlockSpec((None, POOL_PAD, D_MODEL),
                         lambda i, t: (i, jnp.maximum(t * halo_per_tile - 1, 0), 0)),
            pl.BlockSpec((None, POOL_PAD, D_MODEL),
                         lambda i, t: (i, jnp.minimum((t + 1) * halo_per_tile, last_halo), 0)),
            _resident((None, D_MODEL, IN_PROJ_WIDTH), lambda i, t: (layer, 0, 0)),
            pl.BlockSpec((None, POOL_WIDTH, POOL_WIDTH), lambda i, t: (layer, 0, 0)),
            pl.BlockSpec((None, 1, POOL_WIDTH), lambda i, t: (layer, 0, 0)),
            table, table,
        ],
        out_specs=[
            pl.BlockSpec((None, ROW_TILE, POOL_WIDTH), lambda i, t: (i, t, 0)),
            pl.BlockSpec((None, ROW_TILE, QKV_WIDTH), lambda i, t: (i, t, 0)),
            pl.BlockSpec((None, MID_DIL, ROW_TILE // MID_DIL, QKV_WIDTH),
                         lambda i, t: (i, 0, t, 0)),
            pl.BlockSpec((None, far_dil, ROW_TILE // far_dil, QKV_WIDTH),
                         lambda i, t: (i, 0, t, 0)),
        ],
        out_shape=[
            jax.ShapeDtypeStruct((b, s, POOL_WIDTH), _BF16),
            jax.ShapeDtypeStruct((b, s, QKV_WIDTH), _BF16),
            jax.ShapeDtypeStruct((b, MID_DIL, s // MID_DIL, QKV_WIDTH), _BF16),
            jax.ShapeDtypeStruct((b, far_dil, s // far_dil, QKV_WIDTH), _BF16),
        ],
        scratch_shapes=[
            pltpu.VMEM((QKV_WIDTH // LANES, ROW_TILE, LANES), _F32),
            pltpu.VMEM((QKV_WIDTH // LANES, MID_DIL, ROW_TILE // MID_DIL, LANES), _F32),
            pltpu.VMEM((ROW_TILE + 2 * POOL_PAD, POOL_WIDTH), _F32),
            pltpu.VMEM((tiles, ROW_TILE, POOL_WIDTH), _F32)],
        compiler_params=_params(2),
        name="inproj",
    )(h, h, h, w_in, pool_w_bd, pool_scale, cos, sin)


def _attn_kernel(qn_ref, kn_ref, vn_ref, qm_ref, km_ref, vm_ref, qf_ref, kf_ref, vf_ref, o_ref,
                 mask_ref, s_ref, p_ref, st_ref, num_far, m_far, l_far, num_mid, m_mid, l_mid):
    s_len = qn_ref.shape[1]
    far_dil = MID_DIL * MID_DIL
    operands = ((qn_ref, kn_ref, vn_ref), (qm_ref, km_ref, vm_ref), (qf_ref, kf_ref, vf_ref))
    first = lax.broadcasted_iota(jnp.int32, (Q_BLOCK, LANES), 1) < HEAD_DIM

    rr = lax.broadcasted_iota(jnp.int32, (Q_BLOCK, 2 * Q_BLOCK), 0)
    cc = lax.broadcasted_iota(jnp.int32, (Q_BLOCK, 2 * Q_BLOCK), 1)
    for idx in range(3):
        off = (idx - 2) * BAND_HALF
        mask_ref[idx] = jnp.where(jnp.abs(cc - rr + off) <= BAND_HALF, jnp.inf, MASK_VALUE)

    blocks = []
    for order, n_seq in ((2, far_dil), (1, MID_DIL), (0, 1)):
        seq = s_len // n_seq
        blocks += [(pair, order, r, q0, seq) for r in range(n_seq)
                   for q0 in range(0, seq, Q_BLOCK) for pair in range(ATTN_PAIRS)]

    def key_window(q0, seq):
        kwin = min(2 * Q_BLOCK, seq)
        return min(max(q0 - BAND_HALF, 0), seq - kwin), kwin

    def pair_lanes(pair):
        return slice(pair * LANES, (pair + 1) * LANES)

    def scores(blk, slot):
        pair, order, r, q0, seq = blk
        q_ref, k_ref, _ = operands[order]
        k0, kwin = key_window(q0, seq)
        q = q_ref[r, q0:q0 + Q_BLOCK, pair_lanes(pair)]
        kblk = k_ref[r, k0:k0 + kwin, pair_lanes(pair)]
        cap = mask_ref[(k0 - q0) // BAND_HALF + 2, :, 0:kwin]
        zero = jnp.zeros_like(q)
        for h, qh in enumerate((jnp.where(first, q, zero), jnp.where(first, zero, q))):
            sc = lax.dot_general(qh, kblk, (((1,), (1,)), ((), ())),
                                 preferred_element_type=_F32)
            s_ref[slot, h, :, 0:kwin] = jnp.minimum(sc, cap)

    def softmax(blk, slot):
        _, kwin = key_window(blk[3], blk[4])
        stats = []
        for h in range(2):
            sc = s_ref[slot, h, :, 0:kwin]
            m = jnp.max(sc, axis=-1, keepdims=True)
            p = jnp.exp2(sc - m)
            stats.append((m, jnp.sum(p, axis=-1, keepdims=True)))
            p_ref[slot, h, :, 0:kwin] = p.astype(_BF16)
        st_ref[slot, 0] = jnp.where(first, stats[0][0], stats[1][0])
        st_ref[slot, 1] = jnp.where(first, stats[0][1], stats[1][1])

    def values(blk, slot):
        pair, order, r, q0, seq = blk
        v_ref = operands[order][2]
        k0, kwin = key_window(q0, seq)
        vblk = v_ref[r, k0:k0 + kwin, pair_lanes(pair)]
        o = [jnp.dot(p_ref[slot, h, :, 0:kwin], vblk, preferred_element_type=_F32)
             for h in range(2)]
        num = jnp.where(first, o[0], o[1])
        m = st_ref[slot, 0]
        l = st_ref[slot, 1]
        rows = slice(q0, q0 + Q_BLOCK)
        if order == 2:
            out_rows = pl.ds(MID_DIL * q0 + r // MID_DIL, Q_BLOCK, stride=MID_DIL)
            num_far[pair, r % MID_DIL, out_rows, :] = num
            m_far[pair, r % MID_DIL, out_rows, :] = m
            l_far[pair, r % MID_DIL, out_rows, :] = l
            return
        if order == 1:
            m_c, num_c, l_c = (ref[pair, r, rows, :] for ref in (m_far, num_far, l_far))
        else:
            m_c, num_c, l_c = (ref[pair, rows, :] for ref in (m_mid, num_mid, l_mid))
        m_new = jnp.maximum(m, m_c)
        a = jnp.exp2(m - m_new)
        a_c = jnp.exp2(m_c - m_new)
        num = a * num + a_c * num_c
        l = a * l + a_c * l_c
        if order == 1:
            out_rows = pl.ds(MID_DIL * q0 + r, Q_BLOCK, stride=MID_DIL)
            num_mid[pair, out_rows, :] = num
            m_mid[pair, out_rows, :] = m_new
            l_mid[pair, out_rows, :] = l
        else:
            o_ref[rows, pair_lanes(pair)] = (num / l).astype(o_ref.dtype)

    stages = (scores, softmax, values)
    for step in range(len(blocks) + len(stages) - 1):
        for depth, stage in enumerate(stages):
            i = step - depth
            if 0 <= i < len(blocks):
                stage(blocks[i], i % ATTN_SLOTS)


def _attention(nat, mid, far):
    b, s, _ = nat.shape
    n_pairs = ATTN_WIDTH // LANES
    far_dil = MID_DIL * MID_DIL

    steps = n_pairs // ATTN_PAIRS
    width = ATTN_PAIRS * LANES

    def specs(n_seq):
        def tile(off):
            return pl.BlockSpec((None, n_seq, s // n_seq, width),
                                lambda i, j: (i, 0, 0, off + j))
        return [tile(0), tile(steps), tile(2 * steps)]

    return pl.pallas_call(
        _attn_kernel,
        grid=(b, steps),
        in_specs=specs(1) + specs(MID_DIL) + specs(far_dil),
        out_specs=pl.BlockSpec((None, s, width), lambda i, j: (i, 0, j)),
        out_shape=jax.ShapeDtypeStruct((b, s, ATTN_WIDTH), _BF16),
        scratch_shapes=[pltpu.VMEM((3, Q_BLOCK, 2 * Q_BLOCK), _F32),
                        pltpu.VMEM((ATTN_SLOTS, 2, Q_BLOCK, 2 * Q_BLOCK), _F32),
                        pltpu.VMEM((ATTN_SLOTS, 2, Q_BLOCK, 2 * Q_BLOCK), _BF16),
                        pltpu.VMEM((ATTN_SLOTS, 2, Q_BLOCK, LANES), _F32)]
        + [pltpu.VMEM((ATTN_PAIRS, MID_DIL, s // MID_DIL, LANES), _F32)] * 3
        + [pltpu.VMEM((ATTN_PAIRS, s, LANES), _F32)] * 3,
        compiler_params=_params(2),
        name="dilated_attention",
    )(nat.reshape(b, 1, s, QKV_WIDTH), nat.reshape(b, 1, s, QKV_WIDTH),
      nat.reshape(b, 1, s, QKV_WIDTH), mid, mid, mid, far, far, far)


def _pool_block_diag(pool_w):
    g = POOL_GROUP_DIM
    z = jnp.zeros((DEPTH, POOL_WIDTH, POOL_WIDTH), pool_w.dtype)
    for k in range(POOL_WIDTH // g):
        z = z.at[:, k * g:(k + 1) * g, k * g:(k + 1) * g].set(pool_w[:, k])
    return z.astype(_BF16)


@jax.jit
def _forward(x, positions, ffn1_norm, ffn1_w_gate, ffn1_w_up, ffn1_w_down, mix_norm, w_in,
             pool_w, pool_scale, w_out, ffn2_norm, ffn2_w_gate, ffn2_w_up, ffn2_w_down,
             final_norm):
    b, s, d = x.shape
    n = b * s
    assert d == D_MODEL and s % ROW_TILE == 0 and s % (Q_BLOCK * MID_DIL * MID_DIL) == 0
    assert all(w // 2 == POOL_WINDOWS[0] * 2 ** i // 2 for i, w in enumerate(POOL_WINDOWS))

    pool_w_bd = _pool_block_diag(pool_w)
    g1 = ffn1_norm.reshape(DEPTH, 1, D_MODEL)
    g2 = ffn2_norm.reshape(DEPTH, 1, D_MODEL)
    gm = mix_norm.reshape(DEPTH, 1, D_MODEL)
    ps = pool_scale.reshape(DEPTH, 1, POOL_WIDTH)
    gf = final_norm.reshape(1, 1, D_MODEL)

    inv_freq = ROPE_THETA ** (-jnp.arange(0, ROPE_DIM, 2, dtype=_F32) / ROPE_DIM)
    invf = jnp.tile(inv_freq, LANES // (ROPE_DIM // 2)).reshape(1, LANES)
    pos = positions.astype(_F32)[..., None]
    cos, sin = _rope_tables(pos, invf)

    xs = x.reshape(n, D_MODEL)
    for layer in range(DEPTH):
        xs, h_mix = _ffn(xs, g1, ffn1_w_gate, ffn1_w_up, ffn1_w_down, layer,
                         tail="emit_norm", tail_gain=(gm, layer))
        y_pool, nat, mid, far = _inproj(h_mix.reshape(b, s, D_MODEL), w_in, pool_w_bd, ps, cos,
                                        sin, layer)
        y_attn = _attention(nat, mid, far)
        mixer = (y_pool.reshape(n, POOL_WIDTH), y_attn.reshape(n, ATTN_WIDTH), w_out)
        last = layer == DEPTH - 1
        xs, = _ffn(xs, g2, ffn2_w_gate, ffn2_w_up, ffn2_w_down, layer, mixer=mixer,
                   tail="final_norm" if last else None, tail_gain=(gf, 0) if last else None)
    return xs.reshape(b, s, D_MODEL)


def kernel(x, positions, ffn1_norm, ffn1_w_gate, ffn1_w_up, ffn1_w_down, mix_norm, w_in,
           pool_w, pool_scale, w_out, ffn2_norm, ffn2_w_gate, ffn2_w_up, ffn2_w_down,
           final_norm):
    return _forward(x, positions, ffn1_norm, ffn1_w_gate, ffn1_w_up, ffn1_w_down, mix_norm,
                    w_in, pool_w, pool_scale, w_out, ffn2_norm, ffn2_w_gate, ffn2_w_up,
                    ffn2_w_down, final_norm)
```

```python
import functools

import jax
import jax.numpy as jnp
from jax import lax
from jax.experimental import pallas as pl
from jax.experimental.pallas import tpu as pltpu

D_MODEL = 1024
DEPTH = 4
POOL_WIDTH = 256
POOL_WINDOWS = (2, 4, 8, 16)
POOL_GROUP_DIM = 64
HEAD_DIM = 64
ATTN_WIDTH = 768
DILATED_PATTERNS = ((128, 1), (512, 4), (2048, 16))
ROPE_THETA = 500000.0
ROPE_DIM = 16
D_FF = 2816
IN_PROJ_WIDTH = 2560
NORM_EPS = 1e-6
MASK_VALUE = -1e30
LOG2_E = 1.4426950408889634

LANES = 128
MXU_WIDTH = 256
ROW_TILE = 512
FF_CHUNK = MXU_WIDTH
N_FF_CHUNKS = D_FF // FF_CHUNK
Q_BLOCK = 128
BAND_HALF = 64
ATTN_PAIRS = 2
ATTN_SLOTS = 3
MID_DIL = 4
QKV_WIDTH = 3 * ATTN_WIDTH
POOL_PAD = 16
VMEM_LIMIT = 60 * 1024 * 1024

_F32 = jnp.float32
_BF16 = jnp.bfloat16

assert all(w // (2 * d) == BAND_HALF for w, d in DILATED_PATTERNS)
assert [d for _, d in DILATED_PATTERNS] == [1, MID_DIL, MID_DIL * MID_DIL]


def _params(n_grid_dims):
    return pltpu.CompilerParams(
        dimension_semantics=("arbitrary",) * n_grid_dims,
        vmem_limit_bytes=VMEM_LIMIT)


def _resident(block_shape, index_map):
    return pl.BlockSpec(block_shape, index_map, pipeline_mode=pl.Buffered(1))


def _rmsnorm(x, g):
    y = x * lax.rsqrt(jnp.mean(x * x, axis=-1, keepdims=True) + NORM_EPS)
    return y * g


def _ffn_kernel(*refs, layer, mix_in, tail):
    refs = list(refs)
    x_ref = refs.pop(0)
    if mix_in:
        yp_ref, ya_ref, wo_ref = refs[:3]
        del refs[:3]
    g_ref, wg_hbm, wu_hbm, wd_hbm = refs[:4]
    del refs[:4]
    tg_ref = refs.pop(0) if tail else None
    o_ref = refs.pop(0)
    hn_ref = refs.pop(0) if tail == "emit_norm" else None
    h_ref, acc_ref, wg_ref, wu_ref, wd_ref, sem = refs

    def weight_copy(kind, c):
        cols = pl.ds(c * FF_CHUNK, FF_CHUNK)
        src, dst = ((wg_hbm.at[layer, :, cols], wg_ref.at[c]),
                    (wu_hbm.at[layer, :, cols], wu_ref.at[c]),
                    (wd_hbm.at[layer, cols, :], wd_ref.at[c]))[kind]
        return pltpu.make_async_copy(src, dst, sem.at[kind, c])

    mixer_refs = (yp_ref, ya_ref, wo_ref) if mix_in else None
    step = functools.partial(_ffn_step, x_ref, mixer_refs, g_ref, wg_ref, wu_ref, wd_ref, tg_ref,
                             o_ref, hn_ref, h_ref, acc_ref, tail=tail)
    first_step = pl.program_id(0) == 0

    @pl.when(first_step)
    def _():
        for c in range(N_FF_CHUNKS):
            for kind in range(3):
                weight_copy(kind, c).start()
        step(weight_copy)

    @pl.when(jnp.logical_not(first_step))
    def _():
        step(None)


def _ffn_step(x_ref, mixer_refs, g_ref, wg_ref, wu_ref, wd_ref, tg_ref, o_ref, hn_ref, h_ref,
              acc_ref, weight_copy, *, tail):
    mix_in = mixer_refs is not None
    if mix_in:
        yp_ref, ya_ref, wo_ref = mixer_refs
    x = x_ref[...]
    if mix_in:
        x = x + jnp.dot(yp_ref[...], wo_ref[0:POOL_WIDTH, :].astype(_BF16),
                        preferred_element_type=_F32)
        x = x + jnp.dot(ya_ref[...], wo_ref[POOL_WIDTH:D_MODEL, :].astype(_BF16),
                        preferred_element_type=_F32)
        o_ref[...] = x
    h_ref[...] = _rmsnorm(x, g_ref[...]).astype(_BF16)

    for c in range(N_FF_CHUNKS):
        if weight_copy is not None:
            for kind in range(3):
                weight_copy(kind, c).wait()
        h = h_ref[...]
        gate = jnp.dot(h, wg_ref[c].astype(_BF16), preferred_element_type=_F32)
        up = jnp.dot(h, wu_ref[c].astype(_BF16), preferred_element_type=_F32)
        act = ((gate * jax.nn.sigmoid(gate)) * up).astype(_BF16)
        down = jnp.dot(act, wd_ref[c].astype(_BF16), preferred_element_type=_F32)
        if c == 0:
            acc_ref[...] = down
        elif c < N_FF_CHUNKS - 1:
            acc_ref[...] += down
    base = o_ref[...] if mix_in else x_ref[...]
    y = base + 0.5 * (acc_ref[...] + down)
    if tail == "final_norm":
        y = _rmsnorm(y, tg_ref[...])
    o_ref[...] = y
    if tail == "emit_norm":
        hn_ref[...] = _rmsnorm(y, tg_ref[...]).astype(_BF16)


def _ffn(x, gains, wg, wu, wd, layer, mixer=None, tail=None, tail_gain=None):
    n = x.shape[0]
    rows = lambda width: pl.BlockSpec((ROW_TILE, width), lambda i: (i, 0))
    operands = [x]
    in_specs = [rows(D_MODEL)]
    if mixer is not None:
        operands += list(mixer)
        in_specs += [rows(POOL_WIDTH), rows(ATTN_WIDTH),
                     _resident((None, D_MODEL, D_MODEL), lambda i: (layer, 0, 0))]
    operands += [gains, wg, wu, wd]
    in_specs += [pl.BlockSpec((None, 1, D_MODEL), lambda i: (layer, 0, 0))]
    in_specs += [pl.BlockSpec(memory_space=pl.ANY)] * 3
    out_specs = [rows(D_MODEL)]
    out_shape = [jax.ShapeDtypeStruct((n, D_MODEL), _F32)]
    if tail is not None:
        gain_stack, gain_index = tail_gain
        operands.append(gain_stack)
        in_specs.append(pl.BlockSpec((None, 1, D_MODEL), lambda i: (gain_index, 0, 0)))
    if tail == "emit_norm":
        out_specs.append(rows(D_MODEL))
        out_shape.append(jax.ShapeDtypeStruct((n, D_MODEL), _BF16))
    return pl.pallas_call(
        functools.partial(_ffn_kernel, layer=layer, mix_in=mixer is not None, tail=tail),
        grid=(n // ROW_TILE,),
        in_specs=in_specs,
        out_specs=out_specs,
        out_shape=out_shape,
        scratch_shapes=[pltpu.VMEM((ROW_TILE, D_MODEL), _BF16),
                        pltpu.VMEM((ROW_TILE, D_MODEL), _F32),
                        pltpu.VMEM((N_FF_CHUNKS, D_MODEL, FF_CHUNK), _F32),
                        pltpu.VMEM((N_FF_CHUNKS, D_MODEL, FF_CHUNK), _F32),
                        pltpu.VMEM((N_FF_CHUNKS, FF_CHUNK, D_MODEL), _F32),
                        pltpu.SemaphoreType.DMA((3, N_FF_CHUNKS))],
        compiler_params=_params(1),
        name="ffn",
    )(*operands)


def _rope_table_kernel(pos_ref, invf_ref, cos_ref, sin_ref):
    n_batch = pos_ref.shape[0]
    shape = (pos_ref.shape[1], LANES)
    lane = lax.broadcasted_iota(jnp.int32, shape, 1)
    owner = jnp.right_shift(lane, ROPE_DIM.bit_length() - 1)
    pos = jnp.zeros(shape, _F32)
    for b in range(n_batch):
        pos = jnp.where(owner == b, pos_ref[b], pos)
    ang = pos * invf_ref[...]
    c = jnp.cos(ang)
    s = jnp.sin(ang)
    d = lane & (HEAD_DIM - 1)
    for b in range(n_batch):
        to_head0 = (LANES - ROPE_DIM * b) % LANES
        to_head1 = (HEAD_DIM - ROPE_DIM * b) % LANES

        def spread(t):
            return jnp.where(lane < HEAD_DIM, pltpu.roll(t, to_head0, 1),
                             pltpu.roll(t, to_head1, 1))

        cos_ref[b] = jnp.where(d < ROPE_DIM, spread(c), 1.0)
        sb = spread(s)
        sin_ref[b] = jnp.where(d < ROPE_DIM // 2, -sb, jnp.where(d < ROPE_DIM, sb, 0.0))


def _rope_tables(pos, invf):
    b, s, _ = pos.shape
    assert b * ROPE_DIM == LANES and ROPE_DIM & (ROPE_DIM - 1) == 0
    out = jax.ShapeDtypeStruct((b, s, LANES), _F32)
    spec = pl.BlockSpec((b, ROW_TILE, LANES), lambda i: (0, i, 0))
    return pl.pallas_call(
        _rope_table_kernel,
        grid=(s // ROW_TILE,),
        in_specs=[pl.BlockSpec((b, ROW_TILE, 1), lambda i: (0, i, 0)),
                  pl.BlockSpec((1, LANES), lambda i: (0, 0))],
        out_specs=[spec, spec],
        out_shape=[out, out],
        compiler_params=_params(1),
        name="rope_tables",
    )(pos, invf)


def _pool_head_group(h_ref, hprev_ref, hnext_ref, w_ref, pw_ref, ps_ref, ypool_ref,
                     pad_ref, cnt_ref):
    tile = pl.program_id(1)
    last_tile = pl.num_programs(1) - 1
    g = POOL_GROUP_DIM

    @pl.when(pl.program_id(0) == 0)
    def _():
        col = lax.broadcasted_iota(jnp.int32, (ROW_TILE, POOL_WIDTH), 1)
        half_w = jnp.where(col < g, POOL_WINDOWS[0] // 2,
                           jnp.where(col < 2 * g, POOL_WINDOWS[1] // 2,
                                     jnp.where(col < 3 * g, POOL_WINDOWS[2] // 2,
                                               POOL_WINDOWS[3] // 2)))
        pos = lax.broadcasted_iota(jnp.int32, (ROW_TILE, POOL_WIDTH), 0) + tile * ROW_TILE
        seq_len = pl.num_programs(1) * ROW_TILE
        lo = jnp.maximum(pos - half_w, 0)
        hi = jnp.minimum(pos + half_w - 1, seq_len - 1)
        cnt_ref[tile] = (hi - lo + 1).astype(_F32)

    w_pool = w_ref[:, 0:POOL_WIDTH].astype(_BF16)
    v = jnp.dot(h_ref[...], w_pool, preferred_element_type=_F32)
    v_prev = jnp.dot(hprev_ref[...], w_pool, preferred_element_type=_F32)
    v_next = jnp.dot(hnext_ref[...], w_pool, preferred_element_type=_F32)
    pad_ref[0:POOL_PAD, :] = jnp.where(tile > 0, v_prev, 0.0)
    pad_ref[POOL_PAD:POOL_PAD + ROW_TILE, :] = v
    pad_ref[POOL_PAD + ROW_TILE:, :] = jnp.where(tile < last_tile, v_next, 0.0)
    x = pad_ref[...]
    rows = x.shape[0]

    def shifted(t, k):
        return pltpu.roll(t, k % rows, 0)

    w2 = x + shifted(x, 1)
    w4 = shifted(w2, 1) + shifted(w2, -1)
    w8 = shifted(w4, 2) + shifted(w4, -2)
    w16 = shifted(w8, 4) + shifted(w8, -4)
    lane = lax.broadcasted_iota(jnp.int32, x.shape, 1)
    wsum = jnp.where(lane < g, w2, jnp.where(lane < 2 * g, w4, jnp.where(lane < 3 * g, w8, w16)))
    wsum = wsum[POOL_PAD:POOL_PAD + ROW_TILE, :]
    diff = wsum / cnt_ref[tile] - v
    y = jnp.dot(diff.astype(_BF16), pw_ref[...], preferred_element_type=_F32)
    ypool_ref[...] = (y * ps_ref[...]).astype(ypool_ref.dtype)


def _inproj_kernel(h_ref, hprev_ref, hnext_ref, w_ref, pw_ref, ps_ref, cos_ref, sin_ref,
                   ypool_ref, nat_ref, mid_ref, far_ref,
                   slab_ref, midslab_ref, pad_ref, cnt_ref):
    _pool_head_group(h_ref, hprev_ref, hnext_ref, w_ref, pw_ref, ps_ref, ypool_ref,
                     pad_ref, cnt_ref)
    h = h_ref[...]
    cos = cos_ref[...]
    sin = sin_ref[...]
    half_rope = ROPE_DIM // 2
    n_heads_tiles = ATTN_WIDTH // LANES
    on_t1 = (lax.broadcasted_iota(jnp.int32, cos.shape, 1) & (HEAD_DIM - 1)) < half_rope

    def rope(t):
        partner = jnp.where(on_t1, pltpu.roll(t, LANES - half_rope, 1), pltpu.roll(t, half_rope, 1))
        return t * cos + partner * sin

    for c in range(POOL_WIDTH // MXU_WIDTH, IN_PROJ_WIDTH // MXU_WIDTH):
        cols = slice(c * MXU_WIDTH, (c + 1) * MXU_WIDTH)
        part = jnp.dot(h, w_ref[:, cols].astype(_BF16), preferred_element_type=_F32)
        for half in range(MXU_WIDTH // LANES):
            j = (c * MXU_WIDTH - POOL_WIDTH) // LANES + half
            t = part[:, half * LANES:(half + 1) * LANES]
            if j < n_heads_tiles:
                t = rope(t) * (LOG2_E / (HEAD_DIM ** 0.5))
            elif j < 2 * n_heads_tiles:
                t = rope(t)
            slab_ref[j] = t
            nat_ref[:, j * LANES:(j + 1) * LANES] = t.astype(_BF16)

    far_dil = MID_DIL * MID_DIL
    for j in range(QKV_WIDTH // LANES):
        lanes = slice(j * LANES, (j + 1) * LANES)
        for r in range(MID_DIL):
            rows = pl.ds(r, ROW_TILE // MID_DIL, stride=MID_DIL)
            picked = slab_ref[j, rows, :]
            mid_ref[r, :, lanes] = picked.astype(_BF16)
            midslab_ref[j, r] = picked
        for r in range(far_dil):
            rows = pl.ds(r // MID_DIL, ROW_TILE // far_dil, stride=MID_DIL)
            far_ref[r, :, lanes] = midslab_ref[j, r % MID_DIL, rows, :].astype(_BF16)


def _inproj(h, w_in, pool_w_bd, pool_scale, cos, sin, layer):
    b, s, _ = h.shape
    tiles = s // ROW_TILE
    far_dil = MID_DIL * MID_DIL
    halo_per_tile = ROW_TILE // POOL_PAD
    last_halo = s // POOL_PAD - 1
    table = pl.BlockSpec((None, ROW_TILE, LANES), lambda i, t: (i, t, 0))
    return pl.pallas_call(
        _inproj_kernel,
        grid=(b, tiles),
        in_specs=[
            pl.BlockSpec((None, ROW_TILE, D_MODEL), lambda i, t: (i, t, 0)),
            pl.BlockSpec((None, POOL_PAD, D_MODEL),
                         lambda i, t: (i, jnp.maximum(t * halo_per_tile - 1, 0), 0)),
            pl.BlockSpec((None, POOL_PAD, D_MODEL),
                         lambda i, t: (i, jnp.minimum((t + 1) * halo_per_tile, last_halo), 0)),
            _resident((None, D_MODEL, IN_PROJ_WIDTH), lambda i, t: (layer, 0, 0)),
            pl.BlockSpec((None, POOL_WIDTH, POOL_WIDTH), lambda i, t: (layer, 0, 0)),
            pl.BlockSpec((None, 1, POOL_WIDTH), lambda i, t: (layer, 0, 0)),
            table, table,
        ],
        out_specs=[
            pl.BlockSpec((None, ROW_TILE, POOL_WIDTH), lambda i, t: (i, t, 0)),
            pl.BlockSpec((None, ROW_TILE, QKV_WIDTH), lambda i, t: (i, t, 0)),
            pl.BlockSpec((None, MID_DIL, ROW_TILE // MID_DIL, QKV_WIDTH),
                         lambda i, t: (i, 0, t, 0)),
            pl.BlockSpec((None, far_dil, ROW_TILE // far_dil, QKV_WIDTH),
                         lambda i, t: (i, 0, t, 0)),
        ],
        out_shape=[
            jax.ShapeDtypeStruct((b, s, POOL_WIDTH), _BF16),
            jax.ShapeDtypeStruct((b, s, QKV_WIDTH), _BF16),
            jax.ShapeDtypeStruct((b, MID_DIL, s // MID_DIL, QKV_WIDTH), _BF16),
            jax.ShapeDtypeStruct((b, far_dil, s // far_dil, QKV_WIDTH), _BF16),
        ],
        scratch_shapes=[
            pltpu.VMEM((QKV_WIDTH // LANES, ROW_TILE, LANES), _F32),
            pltpu.VMEM((QKV_WIDTH // LANES, MID_DIL, ROW_TILE // MID_DIL, LANES), _F32),
            pltpu.VMEM((ROW_TILE + 2 * POOL_PAD, POOL_WIDTH), _F32),
            pltpu.VMEM((tiles, ROW_TILE, POOL_WIDTH), _F32)],
        compiler_params=_params(2),
        name="inproj",
    )(h, h, h, w_in, pool_w_bd, pool_scale, cos, sin)


def _attn_kernel(qn_ref, kn_ref, vn_ref, qm_ref, km_ref, vm_ref, qf_ref, kf_ref, vf_ref, o_ref,
                 mask_ref, s_ref, p_ref, st_ref, num_far, m_far, l_far, num_mid, m_mid, l_mid):
    s_len = qn_ref.shape[1]
    far_dil = MID_DIL * MID_DIL
    operands = ((qn_ref, kn_ref, vn_ref), (qm_ref, km_ref, vm_ref), (qf_ref, kf_ref, vf_ref))
    first = lax.broadcasted_iota(jnp.int32, (Q_BLOCK, LANES), 1) < HEAD_DIM

    rr = lax.broadcasted_iota(jnp.int32, (Q_BLOCK, 2 * Q_BLOCK), 0)
    cc = lax.broadcasted_iota(jnp.int32, (Q_BLOCK, 2 * Q_BLOCK), 1)
    for idx in range(3):
        off = (idx - 2) * BAND_HALF
        mask_ref[idx] = jnp.where(jnp.abs(cc - rr + off) <= BAND_HALF, jnp.inf, MASK_VALUE)

    blocks = []
    for order, n_seq in ((2, far_dil), (1, MID_DIL), (0, 1)):
        seq = s_len // n_seq
        blocks += [(pair, order, r, q0, seq) for r in range(n_seq)
                   for q0 in range(0, seq, Q_BLOCK) for pair in range(ATTN_PAIRS)]

    def key_window(q0, seq):
        kwin = min(2 * Q_BLOCK, seq)
        return min(max(q0 - BAND_HALF, 0), seq - kwin), kwin

    def pair_lanes(pair):
        return slice(pair * LANES, (pair + 1) * LANES)

    def scores(blk, slot):
        pair, order, r, q0, seq = blk
        q_ref, k_ref, _ = operands[order]
        k0, kwin = key_window(q0, seq)
        q = q_ref[r, q0:q0 + Q_BLOCK, pair_lanes(pair)]
        kblk = k_ref[r, k0:k0 + kwin, pair_lanes(pair)]
        cap = mask_ref[(k0 - q0) // BAND_HALF + 2, :, 0:kwin]
        zero = jnp.zeros_like(q)
        for h, qh in enumerate((jnp.where(first, q, zero), jnp.where(first, zero, q))):
            sc = lax.dot_general(qh, kblk, (((1,), (1,)), ((), ())),
                                 preferred_element_type=_F32)
            s_ref[slot, h, :, 0:kwin] = jnp.minimum(sc, cap)

    def softmax(blk, slot):
        _, kwin = key_window(blk[3], blk[4])
        stats = []
        for h in range(2):
            sc = s_ref[slot, h, :, 0:kwin]
            m = jnp.max(sc, axis=-1, keepdims=True)
            p = jnp.exp2(sc - m)
            stats.append((m, jnp.sum(p, axis=-1, keepdims=True)))
            p_ref[slot, h, :, 0:kwin] = p.astype(_BF16)
        st_ref[slot, 0] = jnp.where(first, stats[0][0], stats[1][0])
        st_ref[slot, 1] = jnp.where(first, stats[0][1], stats[1][1])

    def values(blk, slot):
        pair, order, r, q0, seq = blk
        v_ref = operands[order][2]
        k0, kwin = key_window(q0, seq)
        vblk = v_ref[r, k0:k0 + kwin, pair_lanes(pair)]
        o = [jnp.dot(p_ref[slot, h, :, 0:kwin], vblk, preferred_element_type=_F32)
             for h in range(2)]
        num = jnp.where(first, o[0], o[1])
        m = st_ref[slot, 0]
        l = st_ref[slot, 1]
        rows = slice(q0, q0 + Q_BLOCK)
        if order == 2:
            out_rows = pl.ds(MID_DIL * q0 + r // MID_DIL, Q_BLOCK, stride=MID_DIL)
            num_far[pair, r % MID_DIL, out_rows, :] = num
            m_far[pair, r % MID_DIL, out_rows, :] = m
            l_far[pair, r % MID_DIL, out_rows, :] = l
            return
        if order == 1:
            m_c, num_c, l_c = (ref[pair, r, rows, :] for ref in (m_far, num_far, l_far))
        else:
            m_c, num_c, l_c = (ref[pair, rows, :] for ref in (m_mid, num_mid, l_mid))
        m_new = jnp.maximum(m, m_c)
        a = jnp.exp2(m - m_new)
        a_c = jnp.exp2(m_c - m_new)
        num = a * num + a_c * num_c
        l = a * l + a_c * l_c
        if order == 1:
            out_rows = pl.ds(MID_DIL * q0 + r, Q_BLOCK, stride=MID_DIL)
            num_mid[pair, out_rows, :] = num
            m_mid[pair, out_rows, :] = m_new
            l_mid[pair, out_rows, :] = l
        else:
            o_ref[rows, pair_lanes(pair)] = (num / l).astype(o_ref.dtype)

    stages = (scores, softmax, values)
    for step in range(len(blocks) + len(stages) - 1):
        for depth, stage in enumerate(stages):
            i = step - depth
            if 0 <= i < len(blocks):
                stage(blocks[i], i % ATTN_SLOTS)


def _attention(nat, mid, far):
    b, s, _ = nat.shape
    n_pairs = ATTN_WIDTH // LANES
    far_dil = MID_DIL * MID_DIL

    steps = n_pairs // ATTN_PAIRS
    width = ATTN_PAIRS * LANES

    def specs(n_seq):
        def tile(off):
            return pl.BlockSpec((None, n_seq, s // n_seq, width),
                                lambda i, j: (i, 0, 0, off + j))
        return [tile(0), tile(steps), tile(2 * steps)]

    return pl.pallas_call(
        _attn_kernel,
        grid=(b, steps),
        in_specs=specs(1) + specs(MID_DIL) + specs(far_dil),
        out_specs=pl.BlockSpec((None, s, width), lambda i, j: (i, 0, j)),
        out_shape=jax.ShapeDtypeStruct((b, s, ATTN_WIDTH), _BF16),
        scratch_shapes=[pltpu.VMEM((3, Q_BLOCK, 2 * Q_BLOCK), _F32),
                        pltpu.VMEM((ATTN_SLOTS, 2, Q_BLOCK, 2 * Q_BLOCK), _F32),
                        pltpu.VMEM((ATTN_SLOTS, 2, Q_BLOCK, 2 * Q_BLOCK), _BF16),
                        pltpu.VMEM((ATTN_SLOTS, 2, Q_BLOCK, LANES), _F32)]
        + [pltpu.VMEM((ATTN_PAIRS, MID_DIL, s // MID_DIL, LANES), _F32)] * 3
        + [pltpu.VMEM((ATTN_PAIRS, s, LANES), _F32)] * 3,
        compiler_params=_params(2),
        name="dilated_attention",
    )(nat.reshape(b, 1, s, QKV_WIDTH), nat.reshape(b, 1, s, QKV_WIDTH),
      nat.reshape(b, 1, s, QKV_WIDTH), mid, mid, mid, far, far, far)


def _pool_block_diag(pool_w):
    g = POOL_GROUP_DIM
    z = jnp.zeros((DEPTH, POOL_WIDTH, POOL_WIDTH), pool_w.dtype)
    for k in range(POOL_WIDTH // g):
        z = z.at[:, k * g:(k + 1) * g, k * g:(k + 1) * g].set(pool_w[:, k])
    return z.astype(_BF16)


@jax.jit
def _forward(x, positions, ffn1_norm, ffn1_w_gate, ffn1_w_up, ffn1_w_down, mix_norm, w_in,
             pool_w, pool_scale, w_out, ffn2_norm, ffn2_w_gate, ffn2_w_up, ffn2_w_down,
             final_norm):
    b, s, d = x.shape
    n = b * s
    assert d == D_MODEL and s % ROW_TILE == 0 and s % (Q_BLOCK * MID_DIL * MID_DIL) == 0
    assert all(w // 2 == POOL_WINDOWS[0] * 2 ** i // 2 for i, w in enumerate(POOL_WINDOWS))

    pool_w_bd = _pool_block_diag(pool_w)
    g1 = ffn1_norm.reshape(DEPTH, 1, D_MODEL)
    g2 = ffn2_norm.reshape(DEPTH, 1, D_MODEL)
    gm = mix_norm.reshape(DEPTH, 1, D_MODEL)
    ps = pool_scale.reshape(DEPTH, 1, POOL_WIDTH)
    gf = final_norm.reshape(1, 1, D_MODEL)

    inv_freq = ROPE_THETA ** (-jnp.arange(0, ROPE_DIM, 2, dtype=_F32) / ROPE_DIM)
    invf = jnp.tile(inv_freq, LANES // (ROPE_DIM // 2)).reshape(1, LANES)
    pos = positions.astype(_F32)[..., None]
    cos, sin = _rope_tables(pos, invf)

    xs = x.reshape(n, D_MODEL)
    for layer in range(DEPTH):
        xs, h_mix = _ffn(xs, g1, ffn1_w_gate, ffn1_w_up, ffn1_w_down, layer,
                         tail="emit_norm", tail_gain=(gm, layer))
        y_pool, nat, mid, far = _inproj(h_mix.reshape(b, s, D_MODEL), w_in, pool_w_bd, ps, cos,
                                        sin, layer)
        y_attn = _attention(nat, mid, far)
        mixer = (y_pool.reshape(n, POOL_WIDTH), y_attn.reshape(n, ATTN_WIDTH), w_out)
        last = layer == DEPTH - 1
        xs, = _ffn(xs, g2, ffn2_w_gate, ffn2_w_up, ffn2_w_down, layer, mixer=mixer,
                   tail="final_norm" if last else None, tail_gain=(gf, 0) if last else None)
    return xs.reshape(b, s, D_MODEL)


def kernel(x, positions, ffn1_norm, ffn1_w_gate, ffn1_w_up, ffn1_w_down, mix_norm, w_in,
           pool_w, pool_scale, w_out, ffn2_norm, ffn2_w_gate, ffn2_w_up, ffn2_w_down,
           final_norm):
    return _forward(x, positions, ffn1_norm, ffn1_w_gate, ffn1_w_up, ffn1_w_down, mix_norm,
                    w_in, pool_w, pool_scale, w_out, ffn2_norm, ffn2_w_gate, ffn2_w_up,
                    ffn2_w_down, final_norm)
```

```python
import functools

import jax
import jax.numpy as jnp
from jax import lax
from jax.experimental import pallas as pl
from jax.experimental.pallas import tpu as pltpu

D_MODEL = 1024
DEPTH = 4
POOL_WIDTH = 256
POOL_WINDOWS = (2, 4, 8, 16)
POOL_GROUP_DIM = 64
HEAD_DIM = 64
ATTN_WIDTH = 768
DILATED_PATTERNS = ((128, 1), (512, 4), (2048, 16))
ROPE_THETA = 500000.0
ROPE_DIM = 16
D_FF = 2816
IN_PROJ_WIDTH = 2560
NORM_EPS = 1e-6
MASK_VALUE = -1e30
LOG2_E = 1.4426950408889634

LANES = 128
MXU_WIDTH = 256
ROW_TILE = 512
FF_CHUNK = MXU_WIDTH
N_FF_CHUNKS = D_FF // FF_CHUNK
Q_BLOCK = 128
BAND_HALF = 64
ATTN_PAIRS = 2
ATTN_SLOTS = 3
MID_DIL = 4
QKV_WIDTH = 3 * ATTN_WIDTH
POOL_PAD = 16
VMEM_LIMIT = 60 * 1024 * 1024

_F32 = jnp.float32
_BF16 = jnp.bfloat16

assert all(w // (2 * d) == BAND_HALF for w, d in DILATED_PATTERNS)
assert [d for _, d in DILATED_PATTERNS] == [1, MID_DIL, MID_DIL * MID_DIL]


def _params(n_grid_dims):
    return pltpu.CompilerParams(
        dimension_semantics=("arbitrary",) * n_grid_dims,
        vmem_limit_bytes=VMEM_LIMIT)


def _resident(block_shape, index_map):
    return pl.BlockSpec(block_shape, index_map, pipeline_mode=pl.Buffered(1))


def _rmsnorm(x, g):
    y = x * lax.rsqrt(jnp.mean(x * x, axis=-1, keepdims=True) + NORM_EPS)
    return y * g


def _ffn_kernel(*refs, layer, mix_in, tail):
    refs = list(refs)
    x_ref = refs.pop(0)
    if mix_in:
        yp_ref, ya_ref, wo_ref = refs[:3]
        del refs[:3]
    g_ref, wg_hbm, wu_hbm, wd_hbm = refs[:4]
    del refs[:4]
    tg_ref = refs.pop(0) if tail else None
    o_ref = refs.pop(0)
    hn_ref = refs.pop(0) if tail == "emit_norm" else None
    h_ref, acc_ref, wg_ref, wu_ref, wd_ref, sem = refs

    def weight_copy(kind, c):
        cols = pl.ds(c * FF_CHUNK, FF_CHUNK)
        src, dst = ((wg_hbm.at[layer, :, cols], wg_ref.at[c]),
                    (wu_hbm.at[layer, :, cols], wu_ref.at[c]),
                    (wd_hbm.at[layer, cols, :], wd_ref.at[c]))[kind]
        return pltpu.make_async_copy(src, dst, sem.at[kind, c])

    mixer_refs = (yp_ref, ya_ref, wo_ref) if mix_in else None
    step = functools.partial(_ffn_step, x_ref, mixer_refs, g_ref, wg_ref, wu_ref, wd_ref, tg_ref,
                             o_ref, hn_ref, h_ref, acc_ref, tail=tail)
    first_step = pl.program_id(0) == 0

    @pl.when(first_step)
    def _():
        for c in range(N_FF_CHUNKS):
            for kind in range(3):
                weight_copy(kind, c).start()
        step(weight_copy)

    @pl.when(jnp.logical_not(first_step))
    def _():
        step(None)


def _ffn_step(x_ref, mixer_refs, g_ref, wg_ref, wu_ref, wd_ref, tg_ref, o_ref, hn_ref, h_ref,
              acc_ref, weight_copy, *, tail):
    mix_in = mixer_refs is not None
    if mix_in:
        yp_ref, ya_ref, wo_ref = mixer_refs
    x = x_ref[...]
    if mix_in:
        x = x + jnp.dot(yp_ref[...], wo_ref[0:POOL_WIDTH, :].astype(_BF16),
                        preferred_element_type=_F32)
        x = x + jnp.dot(ya_ref[...], wo_ref[POOL_WIDTH:D_MODEL, :].astype(_BF16),
                        preferred_element_type=_F32)
        o_ref[...] = x
    h_ref[...] = _rmsnorm(x, g_ref[...]).astype(_BF16)

    for c in range(N_FF_CHUNKS):
        if weight_copy is not None:
            for kind in range(3):
                weight_copy(kind, c).wait()
        h = h_ref[...]
        gate = jnp.dot(h, wg_ref[c].astype(_BF16), preferred_element_type=_F32)
        up = jnp.dot(h, wu_ref[c].astype(_BF16), preferred_element_type=_F32)
        act = ((gate * jax.nn.sigmoid(gate)) * up).astype(_BF16)
        down = jnp.dot(act, wd_ref[c].astype(_BF16), preferred_element_type=_F32)
        if c == 0:
            acc_ref[...] = down
        elif c < N_FF_CHUNKS - 1:
            acc_ref[...] += down
    base = o_ref[...] if mix_in else x_ref[...]
    y = base + 0.5 * (acc_ref[...] + down)
    if tail == "final_norm":
        y = _rmsnorm(y, tg_ref[...])
    o_ref[...] = y
    if tail == "emit_norm":
        hn_ref[...] = _rmsnorm(y, tg_ref[...]).astype(_BF16)


def _ffn(x, gains, wg, wu, wd, layer, mixer=None, tail=None, tail_gain=None):
    n = x.shape[0]
    rows = lambda width: pl.BlockSpec((ROW_TILE, width), lambda i: (i, 0))
    operands = [x]
    in_specs = [rows(D_MODEL)]
    if mixer is not None:
        operands += list(mixer)
        in_specs += [rows(POOL_WIDTH), rows(ATTN_WIDTH),
                     _resident((None, D_MODEL, D_MODEL), lambda i: (layer, 0, 0))]
    operands += [gains, wg, wu, wd]
    in_specs += [pl.BlockSpec((None, 1, D_MODEL), lambda i: (layer, 0, 0))]
    in_specs += [pl.BlockSpec(memory_space=pl.ANY)] * 3
    out_specs = [rows(D_MODEL)]
    out_shape = [jax.ShapeDtypeStruct((n, D_MODEL), _F32)]
    if tail is not None:
        gain_stack, gain_index = tail_gain
        operands.append(gain_stack)
        in_specs.append(pl.BlockSpec((None, 1, D_MODEL), lambda i: (gain_index, 0, 0)))
    if tail == "emit_norm":
        out_specs.append(rows(D_MODEL))
        out_shape.append(jax.ShapeDtypeStruct((n, D_MODEL), _BF16))
    return pl.pallas_call(
        functools.partial(_ffn_kernel, layer=layer, mix_in=mixer is not None, tail=tail),
        grid=(n // ROW_TILE,),
        in_specs=in_specs,
        out_specs=out_specs,
        out_shape=out_shape,
        scratch_shapes=[pltpu.VMEM((ROW_TILE, D_MODEL), _BF16),
                        pltpu.VMEM((ROW_TILE, D_MODEL), _F32),
                        pltpu.VMEM((N_FF_CHUNKS, D_MODEL, FF_CHUNK), _F32),
                        pltpu.VMEM((N_FF_CHUNKS, D_MODEL, FF_CHUNK), _F32),
                        pltpu.VMEM((N_FF_CHUNKS, FF_CHUNK, D_MODEL), _F32),
                        pltpu.SemaphoreType.DMA((3, N_FF_CHUNKS))],
        compiler_params=_params(1),
        name="ffn",
    )(*operands)


def _rope_table_kernel(pos_ref, invf_ref, cos_ref, sin_ref):
    n_batch = pos_ref.shape[0]
    shape = (pos_ref.shape[1], LANES)
    lane = lax.broadcasted_iota(jnp.int32, shape, 1)
    owner = jnp.right_shift(lane, ROPE_DIM.bit_length() - 1)
    pos = jnp.zeros(shape, _F32)
    for b in range(n_batch):
        pos = jnp.where(owner == b, pos_ref[b], pos)
    ang = pos * invf_ref[...]
    c = jnp.cos(ang)
    s = jnp.sin(ang)
    d = lane & (HEAD_DIM - 1)
    for b in range(n_batch):
        to_head0 = (LANES - ROPE_DIM * b) % LANES
        to_head1 = (HEAD_DIM - ROPE_DIM * b) % LANES

        def spread(t):
            return jnp.where(lane < HEAD_DIM, pltpu.roll(t, to_head0, 1),
                             pltpu.roll(t, to_head1, 1))

        cos_ref[b] = jnp.where(d < ROPE_DIM, spread(c), 1.0)
        sb = spread(s)
        sin_ref[b] = jnp.where(d < ROPE_DIM // 2, -sb, jnp.where(d < ROPE_DIM, sb, 0.0))


def _rope_tables(pos, invf):
    b, s, _ = pos.shape
    assert b * ROPE_DIM == LANES and ROPE_DIM & (ROPE_DIM - 1) == 0
    out = jax.ShapeDtypeStruct((b, s, LANES), _F32)
    spec = pl.BlockSpec((b, ROW_TILE, LANES), lambda i: (0, i, 0))
    return pl.pallas_call(
        _rope_table_kernel,
        grid=(s // ROW_TILE,),
        in_specs=[pl.BlockSpec((b, ROW_TILE, 1), lambda i: (0, i, 0)),
                  pl.BlockSpec((1, LANES), lambda i: (0, 0))],
        out_specs=[spec, spec],
        out_shape=[out, out],
        compiler_params=_params(1),
        name="rope_tables",
    )(pos, invf)


def _pool_head_group(h_ref, hprev_ref, hnext_ref, w_ref, pw_ref, ps_ref, ypool_ref,
                     pad_ref, cnt_ref):
    tile = pl.program_id(1)
    last_tile = pl.num_programs(1) - 1
    g = POOL_GROUP_DIM

    @pl.when(pl.program_id(0) == 0)
    def _():
        col = lax.broadcasted_iota(jnp.int32, (ROW_TILE, POOL_WIDTH), 1)
        half_w = jnp.where(col < g, POOL_WINDOWS[0] // 2,
                           jnp.where(col < 2 * g, POOL_WINDOWS[1] // 2,
                                     jnp.where(col < 3 * g, POOL_WINDOWS[2] // 2,
                                               POOL_WINDOWS[3] // 2)))
        pos = lax.broadcasted_iota(jnp.int32, (ROW_TILE, POOL_WIDTH), 0) + tile * ROW_TILE
        seq_len = pl.num_programs(1) * ROW_TILE
        lo = jnp.maximum(pos - half_w, 0)
        hi = jnp.minimum(pos + half_w - 1, seq_len - 1)
        cnt_ref[tile] = (hi - lo + 1).astype(_F32)

    w_pool = w_ref[:, 0:POOL_WIDTH].astype(_BF16)
    v = jnp.dot(h_ref[...], w_pool, preferred_element_type=_F32)
    v_prev = jnp.dot(hprev_ref[...], w_pool, preferred_element_type=_F32)
    v_next = jnp.dot(hnext_ref[...], w_pool, preferred_element_type=_F32)
    pad_ref[0:POOL_PAD, :] = jnp.where(tile > 0, v_prev, 0.0)
    pad_ref[POOL_PAD:POOL_PAD + ROW_TILE, :] = v
    pad_ref[POOL_PAD + ROW_TILE:, :] = jnp.where(tile < last_tile, v_next, 0.0)
    x = pad_ref[...]
    rows = x.shape[0]

    def shifted(t, k):
        return pltpu.roll(t, k % rows, 0)

    w2 = x + shifted(x, 1)
    w4 = shifted(w2, 1) + shifted(w2, -1)
    w8 = shifted(w4, 2) + shifted(w4, -2)
    w16 = shifted(w8, 4) + shifted(w8, -4)
    lane = lax.broadcasted_iota(jnp.int32, x.shape, 1)
    wsum = jnp.where(lane < g, w2, jnp.where(lane < 2 * g, w4, jnp.where(lane < 3 * g, w8, w16)))
    wsum = wsum[POOL_PAD:POOL_PAD + ROW_TILE, :]
    diff = wsum / cnt_ref[tile] - v
    y = jnp.dot(diff.astype(_BF16), pw_ref[...], preferred_element_type=_F32)
    ypool_ref[...] = (y * ps_ref[...]).astype(ypool_ref.dtype)


def _inproj_kernel(h_ref, hprev_ref, hnext_ref, w_ref, pw_ref, ps_ref, cos_ref, sin_ref,
                   ypool_ref, nat_ref, mid_ref, far_ref,
                   slab_ref, midslab_ref, pad_ref, cnt_ref):
    _pool_head_group(h_ref, hprev_ref, hnext_ref, w_ref, pw_ref, ps_ref, ypool_ref,
                     pad_ref, cnt_ref)
    h = h_ref[...]
    cos = cos_ref[...]
    sin = sin_ref[...]
    half_rope = ROPE_DIM // 2
    n_heads_tiles = ATTN_WIDTH // LANES
    on_t1 = (lax.broadcasted_iota(jnp.int32, cos.shape, 1) & (HEAD_DIM - 1)) < half_rope

    def rope(t):
        partner = jnp.where(on_t1, pltpu.roll(t, LANES - half_rope, 1), pltpu.roll(t, half_rope, 1))
        return t * cos + partner * sin

    for c in range(POOL_WIDTH // MXU_WIDTH, IN_PROJ_WIDTH // MXU_WIDTH):
        cols = slice(c * MXU_WIDTH, (c + 1) * MXU_WIDTH)
        part = jnp.dot(h, w_ref[:, cols].astype(_BF16), preferred_element_type=_F32)
        for half in range(MXU_WIDTH // LANES):
            j = (c * MXU_WIDTH - POOL_WIDTH) // LANES + half
            t = part[:, half * LANES:(half + 1) * LANES]
            if j < n_heads_tiles:
                t = rope(t) * (LOG2_E / (HEAD_DIM ** 0.5))
            elif j < 2 * n_heads_tiles:
                t = rope(t)
            slab_ref[j] = t
            nat_ref[:, j * LANES:(j + 1) * LANES] = t.astype(_BF16)

    far_dil = MID_DIL * MID_DIL
    for j in range(QKV_WIDTH // LANES):
        lanes = slice(j * LANES, (j + 1) * LANES)
        for r in range(MID_DIL):
            rows = pl.ds(r, ROW_TILE // MID_DIL, stride=MID_DIL)
            picked = slab_ref[j, rows, :]
            mid_ref[r, :, lanes] = picked.astype(_BF16)
            midslab_ref[j, r] = picked
        for r in range(far_dil):
            rows = pl.ds(r // MID_DIL, ROW_TILE // far_dil, stride=MID_DIL)
            far_ref[r, :, lanes] = midslab_ref[j, r % MID_DIL, rows, :].astype(_BF16)


def _inproj(h, w_in, pool_w_bd, pool_scale, cos, sin, layer):
    b, s, _ = h.shape
    tiles = s // ROW_TILE
    far_dil = MID_DIL * MID_DIL
    halo_per_tile = ROW_TILE // POOL_PAD
    last_halo = s // POOL_PAD - 1
    table = pl.BlockSpec((None, ROW_TILE, LANES), lambda i, t: (i, t, 0))
    return pl.pallas_call(
        _inproj_kernel,
        grid=(b, tiles),
        in_specs=[
            pl.BlockSpec((None, ROW_TILE, D_MODEL), lambda i, t: (i, t, 0)),
            pl.BlockSpec((None, POOL_PAD, D_MODEL),
                         lambda i, t: (i, jnp.maximum(t * halo_per_tile - 1, 0), 0)),
            pl.BlockSpec((None, POOL_PAD, D_MODEL),
                         lambda i, t: (i, jnp.minimum((t + 1) * halo_per_tile, last_halo), 0)),
            _resident((None, D_MODEL, IN_PROJ_WIDTH), lambda i, t: (layer, 0, 0)),
            pl.BlockSpec((None, POOL_WIDTH, POOL_WIDTH), lambda i, t: (layer, 0, 0)),
            pl.BlockSpec((None, 1, POOL_WIDTH), lambda i, t: (layer, 0, 0)),
            table, table,
        ],
        out_specs=[
            pl.BlockSpec((None, ROW_TILE, POOL_WIDTH), lambda i, t: (i, t, 0)),
            pl.BlockSpec((None, ROW_TILE, QKV_WIDTH), lambda i, t: (i, t, 0)),
            pl.BlockSpec((None, MID_DIL, ROW_TILE // MID_DIL, QKV_WIDTH),
                         lambda i, t: (i, 0, t, 0)),
            pl.BlockSpec((None, far_dil, ROW_TILE // far_dil, QKV_WIDTH),
                         lambda i, t: (i, 0, t, 0)),
        ],
        out_shape=[
            jax.ShapeDtypeStruct((b, s, POOL_WIDTH), _BF16),
            jax.ShapeDtypeStruct((b, s, QKV_WIDTH), _BF16),
            jax.ShapeDtypeStruct((b, MID_DIL, s // MID_DIL, QKV_WIDTH), _BF16),
            jax.ShapeDtypeStruct((b, far_dil, s // far_dil, QKV_WIDTH), _BF16),
        ],
        scratch_shapes=[
            pltpu.VMEM((QKV_WIDTH // LANES, ROW_TILE, LANES), _F32),
            pltpu.VMEM((QKV_WIDTH // LANES, MID_DIL, ROW_TILE // MID_DIL, LANES), _F32),
            pltpu.VMEM((ROW_TILE + 2 * POOL_PAD, POOL_WIDTH), _F32),
            pltpu.VMEM((tiles, ROW_TILE, POOL_WIDTH), _F32)],
        compiler_params=_params(2),
        name="inproj",
    )(h, h, h, w_in, pool_w_bd, pool_scale, cos, sin)


def _attn_kernel(qn_ref, kn_ref, vn_ref, qm_ref, km_ref, vm_ref, qf_ref, kf_ref, vf_ref, o_ref,
                 mask_ref, s_ref, p_ref, st_ref, num_far, m_far, l_far, num_mid, m_mid, l_mid):
    s_len = qn_ref.shape[1]
    far_dil = MID_DIL * MID_DIL
    operands = ((qn_ref, kn_ref, vn_ref), (qm_ref, km_ref, vm_ref), (qf_ref, kf_ref, vf_ref))
    first = lax.broadcasted_iota(jnp.int32, (Q_BLOCK, LANES), 1) < HEAD_DIM

    rr = lax.broadcasted_iota(jnp.int32, (Q_BLOCK, 2 * Q_BLOCK), 0)
    cc = lax.broadcasted_iota(jnp.int32, (Q_BLOCK, 2 * Q_BLOCK), 1)
    for idx in range(3):
        off = (idx - 2) * BAND_HALF
        mask_ref[idx] = jnp.where(jnp.abs(cc - rr + off) <= BAND_HALF, jnp.inf, MASK_VALUE)

    blocks = []
    for order, n_seq in ((2, far_dil), (1, MID_DIL), (0, 1)):
        seq = s_len // n_seq
        blocks += [(pair, order, r, q0, seq) for r in range(n_seq)
                   for q0 in range(0, seq, Q_BLOCK) for pair in range(ATTN_PAIRS)]

    def key_window(q0, seq):
        kwin = min(2 * Q_BLOCK, seq)
        return min(max(q0 - BAND_HALF, 0), seq - kwin), kwin

    def pair_lanes(pair):
        return slice(pair * LANES, (pair + 1) * LANES)

    def scores(blk, slot):
        pair, order, r, q0, seq = blk
        q_ref, k_ref, _ = operands[order]
        k0, kwin = key_window(q0, seq)
        q = q_ref[r, q0:q0 + Q_BLOCK, pair_lanes(pair)]
        kblk = k_ref[r, k0:k0 + kwin, pair_lanes(pair)]
        cap = mask_ref[(k0 - q0) // BAND_HALF + 2, :, 0:kwin]
        zero = jnp.zeros_like(q)
        for h, qh in enumerate((jnp.where(first, q, zero), jnp.where(first, zero, q))):
            sc = lax.dot_general(qh, kblk, (((1,), (1,)), ((), ())),
                                 preferred_element_type=_F32)
            s_ref[slot, h, :, 0:kwin] = jnp.minimum(sc, cap)

    def softmax(blk, slot):
        _, kwin = key_window(blk[3], blk[4])
        stats = []
        for h in range(2):
            sc = s_ref[slot, h, :, 0:kwin]
            m = jnp.max(sc, axis=-1, keepdims=True)
            p = jnp.exp2(sc - m)
            stats.append(m)
            p_ref[slot, h, :, 0:kwin] = p.astype(_BF16)
        st_ref[slot, 0] = jnp.where(first, stats[0], stats[1])

    def values(blk, slot):
        pair, order, r, q0, seq = blk
        v_ref = operands[order][2]
        k0, kwin = key_window(q0, seq)
        vblk = v_ref[r, k0:k0 + kwin, pair_lanes(pair)]
        v_ones = jnp.concatenate([vblk, jnp.ones_like(vblk)], axis=1)
        o = [jnp.dot(p_ref[slot, h, :, 0:kwin], v_ones, preferred_element_type=_F32)
             for h in range(2)]
        num = jnp.where(first, o[0][:, 0:LANES], o[1][:, 0:LANES])
        l = jnp.where(first, o[0][:, LANES:], o[1][:, LANES:])
        m = st_ref[slot, 0]
        rows = slice(q0, q0 + Q_BLOCK)
        if order == 2:
            out_rows = pl.ds(MID_DIL * q0 + r // MID_DIL, Q_BLOCK, stride=MID_DIL)
            num_far[pair, r % MID_DIL, out_rows, :] = num
            m_far[pair, r % MID_DIL, out_rows, :] = m
            l_far[pair, r % MID_DIL, out_rows, :] = l
            return
        if order == 1:
            m_c, num_c, l_c = (ref[pair, r, rows, :] for ref in (m_far, num_far, l_far))
        else:
            m_c, num_c, l_c = (ref[pair, rows, :] for ref in (m_mid, num_mid, l_mid))
        m_new = jnp.maximum(m, m_c)
        a = jnp.exp2(m - m_new)
        a_c = jnp.exp2(m_c - m_new)
        num = a * num + a_c * num_c
        l = a * l + a_c * l_c
        if order == 1:
            out_rows = pl.ds(MID_DIL * q0 + r, Q_BLOCK, stride=MID_DIL)
            num_mid[pair, out_rows, :] = num
            m_mid[pair, out_rows, :] = m_new
            l_mid[pair, out_rows, :] = l
        else:
            o_ref[rows, pair_lanes(pair)] = (num / l).astype(o_ref.dtype)

    stages = (scores, softmax, values)
    for step in range(len(blocks) + len(stages) - 1):
        for depth, stage in enumerate(stages):
            i = step - depth
            if 0 <= i < len(blocks):
                stage(blocks[i], i % ATTN_SLOTS)


def _attention(nat, mid, far):
    b, s, _ = nat.shape
    n_pairs = ATTN_WIDTH // LANES
    far_dil = MID_DIL * MID_DIL

    steps = n_pairs // ATTN_PAIRS
    width = ATTN_PAIRS * LANES

    def specs(n_seq):
        def tile(off):
            return pl.BlockSpec((None, n_seq, s // n_seq, width),
                                lambda i, j: (i, 0, 0, off + j))
        return [tile(0), tile(steps), tile(2 * steps)]

    return pl.pallas_call(
        _attn_kernel,
        grid=(b, steps),
        in_specs=specs(1) + specs(MID_DIL) + specs(far_dil),
        out_specs=pl.BlockSpec((None, s, width), lambda i, j: (i, 0, j)),
        out_shape=jax.ShapeDtypeStruct((b, s, ATTN_WIDTH), _BF16),
        scratch_shapes=[pltpu.VMEM((3, Q_BLOCK, 2 * Q_BLOCK), _F32),
                        pltpu.VMEM((ATTN_SLOTS, 2, Q_BLOCK, 2 * Q_BLOCK), _F32),
                        pltpu.VMEM((ATTN_SLOTS, 2, Q_BLOCK, 2 * Q_BLOCK), _BF16),
                        pltpu.VMEM((ATTN_SLOTS, 2, Q_BLOCK, LANES), _F32)]
        + [pltpu.VMEM((ATTN_PAIRS, MID_DIL, s // MID_DIL, LANES), _F32)] * 3
        + [pltpu.VMEM((ATTN_PAIRS, s, LANES), _F32)] * 3,
        compiler_params=_params(2),
        name="dilated_attention",
    )(nat.reshape(b, 1, s, QKV_WIDTH), nat.reshape(b, 1, s, QKV_WIDTH),
      nat.reshape(b, 1, s, QKV_WIDTH), mid, mid, mid, far, far, far)


def _pool_block_diag(pool_w):
    g = POOL_GROUP_DIM
    z = jnp.zeros((DEPTH, POOL_WIDTH, POOL_WIDTH), pool_w.dtype)
    for k in range(POOL_WIDTH // g):
        z = z.at[:, k * g:(k + 1) * g, k * g:(k + 1) * g].set(pool_w[:, k])
    return z.astype(_BF16)


@jax.jit
def _forward(x, positions, ffn1_norm, ffn1_w_gate, ffn1_w_up, ffn1_w_down, mix_norm, w_in,
             pool_w, pool_scale, w_out, ffn2_norm, ffn2_w_gate, ffn2_w_up, ffn2_w_down,
             final_norm):
    b, s, d = x.shape
    n = b * s
    assert d == D_MODEL and s % ROW_TILE == 0 and s % (Q_BLOCK * MID_DIL * MID_DIL) == 0
    assert all(w // 2 == POOL_WINDOWS[0] * 2 ** i // 2 for i, w in enumerate(POOL_WINDOWS))

    pool_w_bd = _pool_block_diag(pool_w)
    g1 = ffn1_norm.reshape(DEPTH, 1, D_MODEL)
    g2 = ffn2_norm.reshape(DEPTH, 1, D_MODEL)
    gm = mix_norm.reshape(DEPTH, 1, D_MODEL)
    ps = pool_scale.reshape(DEPTH, 1, POOL_WIDTH)
    gf = final_norm.reshape(1, 1, D_MODEL)

    inv_freq = ROPE_THETA ** (-jnp.arange(0, ROPE_DIM, 2, dtype=_F32) / ROPE_DIM)
    invf = jnp.tile(inv_freq, LANES // (ROPE_DIM // 2)).reshape(1, LANES)
    pos = positions.astype(_F32)[..., None]
    cos, sin = _rope_tables(pos, invf)

    xs = x.reshape(n, D_MODEL)
    for layer in range(DEPTH):
        xs, h_mix = _ffn(xs, g1, ffn1_w_gate, ffn1_w_up, ffn1_w_down, layer,
                         tail="emit_norm", tail_gain=(gm, layer))
        y_pool, nat, mid, far = _inproj(h_mix.reshape(b, s, D_MODEL), w_in, pool_w_bd, ps, cos,
                                        sin, layer)
        y_attn = _attention(nat, mid, far)
        mixer = (y_pool.reshape(n, POOL_WIDTH), y_attn.reshape(n, ATTN_WIDTH), w_out)
        last = layer == DEPTH - 1
        xs, = _ffn(xs, g2, ffn2_w_gate, ffn2_w_up, ffn2_w_down, layer, mixer=mixer,
                   tail="final_norm" if last else None, tail_gain=(gf, 0) if last else None)
    return xs.reshape(b, s, D_MODEL)


def kernel(x, positions, ffn1_norm, ffn1_w_gate, ffn1_w_up, ffn1_w_down, mix_norm, w_in,
           pool_w, pool_scale, w_out, ffn2_norm, ffn2_w_gate, ffn2_w_up, ffn2_w_down,
           final_norm):
    return _forward(x, positions, ffn1_norm, ffn1_w_gate, ffn1_w_up, ffn1_w_down, mix_norm,
                    w_in, pool_w, pool_scale, w_out, ffn2_norm, ffn2_w_gate, ffn2_w_up,
                    ffn2_w_down, final_norm)
```
